```python
import math
import jax, jax.numpy as jnp
from jax import lax
import numpy as np

D_MODEL = 1024
BATCH = 4
SEQ = 4096
DEPTH = 4
DEC_BATCH = 32
DEC_SEQ = 2048
PAST_LEN = 128

RET_HEADS = 8
RET_HEAD_DIM = 64
RET_WIDTH = RET_HEADS * RET_HEAD_DIM
RET_CHUNK = 128
ATT_HEADS = 8
ATT_KV_HEADS = 2
ATT_HEAD_DIM = 64
ATT_WIDTH = ATT_HEADS * ATT_HEAD_DIM
ATT_KV_WIDTH = ATT_KV_HEADS * ATT_HEAD_DIM
WINDOW = 128
ATT_BLOCK = 128
MIX_WIDTH = RET_WIDTH + ATT_WIDTH
IN_WIDTH = 4 * RET_WIDTH + ATT_WIDTH + 2 * ATT_KV_WIDTH
IN_SPLITS = (RET_WIDTH, 2 * RET_WIDTH, 3 * RET_WIDTH, 4 * RET_WIDTH,
             4 * RET_WIDTH + ATT_WIDTH, 4 * RET_WIDTH + ATT_WIDTH + ATT_KV_WIDTH)
N_EXPERTS = 32
TOP_K = 4
D_FF = D_MODEL
SWIGLU_LIMIT = 7.0
SWIGLU_ALPHA = 1.702
MOE_BLOCK = 128
RMS_EPS = 1e-6
NEG_INF = -1e30

kernel_name = "hymba_retention_swa_moe_adaln_encoder"


def rms_norm(x):
    xf = x.astype(jnp.float32)
    return (xf * lax.rsqrt(jnp.mean(xf * xf, axis=-1, keepdims=True) + RMS_EPS)).astype(x.dtype)


def modulate(x, gain, shift, scale):
    return rms_norm(x) * gain * (1.0 + scale[:, None, :]) + shift[:, None, :]


def alibi_slopes(n):
    return 2.0 ** (-8.0 * (jnp.arange(n, dtype=jnp.float32) + 1.0) / n)


def retention_chunkwise(q, k, v, log_gamma, include_diag):
    B, H, S, d = q.shape
    C = RET_CHUNK
    N = S // C
    qc = q.reshape(B, H, N, C, d)
    kc = k.reshape(B, H, N, C, d)
    vc = v.reshape(B, H, N, C, d)
    lg = log_gamma.astype(jnp.float32)
    pos = jnp.arange(C, dtype=jnp.float32)
    diff = pos[:, None] - pos[None, :]
    mask = diff >= 0 if include_diag else diff > 0
    decay = jnp.where(mask, jnp.exp(lg[:, None, None] * jnp.where(mask, diff, 0.0)), 0.0)
    scores = jnp.einsum('bhncd,bhned->bhnce', qc, kc) * decay[None, :, None]
    intra = jnp.einsum('bhnce,bhned->bhncd', scores, vc)
    k_decay = jnp.exp(lg[:, None] * (C - 1.0 - pos))
    q_decay = jnp.exp(lg[:, None] * (pos + 1.0))
    chunk_decay = jnp.exp(lg * C)
    u = jnp.einsum('bhncd,bhnce->bhnde', kc * k_decay[None, :, None, :, None], vc)

    def step(state, u_n):
        return state * chunk_decay[None, :, None, None] + u_n, state

    _, prev = lax.scan(step, jnp.zeros((B, H, d, d), jnp.float32), jnp.moveaxis(u, 2, 0))
    prev = jnp.moveaxis(prev, 0, 2)
    cross = jnp.einsum('bhncd,bhnde->bhnce', qc * q_decay[None, :, None, :, None], prev)
    return (intra + cross).reshape(B, H, S, d)


def window_attention(q, k, v, sink):
    B, Hq, S, d = q.shape
    Hkv = k.shape[1]
    G = Hq // Hkv
    L = ATT_BLOCK
    NB = S // L
    qb = jnp.moveaxis(q.reshape(B, Hkv, G, NB, L, d), 3, 0)

    def band(t):
        tp = jnp.pad(t, ((0, 0), (0, 0), (L, L), (0, 0))).reshape(B, Hkv, NB + 2, L, d)
        tw = jnp.concatenate([tp[:, :, :-2], tp[:, :, 1:-1], tp[:, :, 2:]], axis=3)
        return jnp.moveaxis(tw, 2, 0)

    kb, vb = band(k), band(v)
    blk = jnp.arange(NB)[:, None]
    q_pos = blk * L + jnp.arange(L)[None, :]
    k_pos = blk * L - L + jnp.arange(3 * L)[None, :]
    rel = jnp.abs(q_pos[:, :, None] - k_pos[:, None, :])
    valid = (rel <= WINDOW) & (k_pos[:, None, :] >= 0) & (k_pos[:, None, :] < S)
    dist = rel.astype(jnp.float32)
    slopes = alibi_slopes(Hq).reshape(Hkv, G, 1, 1)
    sink_l = sink.astype(jnp.float32).reshape(Hkv, G, 1, 1)
    scale = d ** -0.5

    def block(args):
        qi, ki, vi, di, mi = args
        s = jnp.einsum('bkgqd,bksd->bkgqs', qi, ki).astype(jnp.float32) * scale - slopes * di
        s = jnp.where(mi, s, NEG_INF)
        sk = jnp.broadcast_to(sink_l, s.shape[:-1] + (1,))
        p = jax.nn.softmax(jnp.concatenate([s, sk], axis=-1), axis=-1)[..., :-1]
        return jnp.einsum('bkgqs,bksd->bkgqd', p.astype(vi.dtype), vi)

    out = lax.map(block, (qb, kb, vb, dist, valid))
    return out.transpose(1, 0, 4, 2, 3, 5).reshape(B, S, Hq * d)


def mixer(h, w_in, lg_f, lg_b, q_g, k_g, sink, w_out):
    B, S, _ = h.shape
    proj = h @ w_in
    q_r, k_r, v_r, g_r, q_a, k_a, v_a = jnp.split(proj, IN_SPLITS, axis=-1)

    def heads(t, n, dh):
        return t.reshape(B, S, n, dh).transpose(0, 2, 1, 3)

    qr = heads(q_r, RET_HEADS, RET_HEAD_DIM).astype(jnp.float32)
    kr = heads(k_r, RET_HEADS, RET_HEAD_DIM).astype(jnp.float32) * (RET_HEAD_DIM ** -0.5)
    vr = heads(v_r, RET_HEADS, RET_HEAD_DIM).astype(jnp.float32)
    o_f = retention_chunkwise(qr, kr, vr, lg_f, True)
    o_b = jnp.flip(retention_chunkwise(jnp.flip(qr, 2), jnp.flip(kr, 2), jnp.flip(vr, 2), lg_b, False), 2)
    o_r = rms_norm(o_f + o_b)
    o_r = o_r.transpose(0, 2, 1, 3).reshape(B, S, RET_WIDTH).astype(h.dtype) * jax.nn.silu(g_r)

    qa = (rms_norm(q_a.reshape(B, S, ATT_HEADS, ATT_HEAD_DIM)) * q_g).transpose(0, 2, 1, 3)
    ka = (rms_norm(k_a.reshape(B, S, ATT_KV_HEADS, ATT_HEAD_DIM)) * k_g).transpose(0, 2, 1, 3)
    va = heads(v_a, ATT_KV_HEADS, ATT_HEAD_DIM)
    o_a = window_attention(qa, ka, va, sink).astype(h.dtype)

    return jnp.concatenate([o_r, o_a], axis=-1) @ w_out


def moe_ffn(h, router_w, router_b, w_gu, b_gu, w_dn, b_dn):
    B, S, D = h.shape
    x = h.reshape(-1, D)
    N = x.shape[0]
    logits = (x @ router_w + router_b).astype(jnp.float32)
    top_v, top_i = lax.top_k(logits, TOP_K)
    gates = jax.nn.softmax(top_v, axis=-1).astype(x.dtype)
    A = N * TOP_K
    flat_e = top_i.reshape(-1)
    flat_tok = jnp.arange(A, dtype=jnp.int32) // TOP_K
    flat_gate = gates.reshape(-1)
    order = jnp.argsort(flat_e)
    sorted_e = flat_e[order]
    counts = jnp.bincount(flat_e, length=N_EXPERTS)
    padded = (counts + MOE_BLOCK - 1) // MOE_BLOCK * MOE_BLOCK
    pad_end = jnp.cumsum(padded)
    pad_start = pad_end - padded
    start = jnp.cumsum(counts) - counts
    dest = pad_start[sorted_e] + jnp.arange(A, dtype=jnp.int32) - start[sorted_e]
    n_blocks = -(-A // MOE_BLOCK) + N_EXPERTS
    R = n_blocks * MOE_BLOCK
    row_tok = jnp.full((R,), N, jnp.int32).at[dest].set(flat_tok[order])
    row_gate = jnp.zeros((R,), x.dtype).at[dest].set(flat_gate[order])
    block_e = jnp.minimum(jnp.searchsorted(pad_end, jnp.arange(n_blocks) * MOE_BLOCK, side='right'),
                          N_EXPERTS - 1)
    x_pad = jnp.concatenate([x, jnp.zeros((1, D), x.dtype)], axis=0)

    def step(acc, blk):
        e, toks, g = blk
        gu = x_pad[toks] @ w_gu[e] + b_gu[e]
        glu = jnp.minimum(gu[:, ::2], SWIGLU_LIMIT)
        lin = jnp.clip(gu[:, 1::2], -SWIGLU_LIMIT, SWIGLU_LIMIT)
        act = (lin + 1.0) * glu * jax.nn.sigmoid(SWIGLU_ALPHA * glu)
        y = act @ w_dn[e] + b_dn[e]
        return acc.at[toks].add((y * g[:, None]).astype(acc.dtype)), None

    acc, _ = lax.scan(step, jnp.zeros((N + 1, D), x.dtype),
                      (block_e, row_tok.reshape(n_blocks, MOE_BLOCK), row_gate.reshape(n_blocks, MOE_BLOCK)))
    return acc[:N].reshape(B, S, D)


def trunk(x, c, ada_w, ada_b, norm_mix_g, w_in, ret_log_gamma_f, ret_log_gamma_b,
          q_norm_g, k_norm_g, attn_sink, w_out, norm_ffn_g, router_w, router_b,
          w_gu, b_gu, w_dn, b_dn):
    c_act = jax.nn.silu(c)
    for l in range(DEPTH):
        mod = c_act @ ada_w[l] + ada_b[l]
        sh1, sc1, g1, sh2, sc2, g2 = jnp.split(mod, 6, axis=-1)
        hm = modulate(x, norm_mix_g[l], sh1, sc1)
        x = x + g1[:, None, :] * mixer(hm, w_in[l], ret_log_gamma_f[l], ret_log_gamma_b[l],
                                       q_norm_g[l], k_norm_g[l], attn_sink[l], w_out[l])
        hf = modulate(x, norm_ffn_g[l], sh2, sc2)
        x = x + g2[:, None, :] * moe_ffn(hf, router_w[l], router_b[l], w_gu[l], b_gu[l], w_dn[l], b_dn[l])
    return x


def setup_inputs(seed: int = 0) -> dict:
    key = jax.random.key(seed)
    ks = jax.random.split(key, 24)
    f32 = jnp.float32

    def nrm(k, shape, s):
        return jax.random.normal(k, shape, f32) * s

    base_lg = jnp.asarray(np.log(1.0 - 2.0 ** (-5.0 - np.arange(RET_HEADS))).astype(np.float32))
    return {
        "x_prompt": nrm(ks[0], (BATCH, SEQ, D_MODEL), 1.0),
        "x_sample": nrm(ks[1], (DEC_BATCH, DEC_SEQ, D_MODEL), 1.0),
        "c_prompt": nrm(ks[2], (BATCH, D_MODEL), 1.0),
        "c_sample": nrm(ks[3], (DEC_BATCH, D_MODEL), 1.0),
        "ada_w": nrm(ks[4], (DEPTH, D_MODEL, 6 * D_MODEL), 0.5 * D_MODEL ** -0.5),
        "ada_b": nrm(ks[5], (DEPTH, 6 * D_MODEL), 0.01),
        "norm_mix_g": 1.0 + nrm(ks[6], (DEPTH, D_MODEL), 0.05),
        "w_in": nrm(ks[7], (DEPTH, D_MODEL, IN_WIDTH), D_MODEL ** -0.5),
        "ret_log_gamma_f": base_lg * jnp.exp(nrm(ks[8], (DEPTH, RET_HEADS), 0.1)),
        "ret_log_gamma_b": base_lg * jnp.exp(nrm(ks[9], (DEPTH, RET_HEADS), 0.1)),
        "q_norm_g": 1.0 + nrm(ks[10], (DEPTH, ATT_HEAD_DIM), 0.05),
        "k_norm_g": 1.0 + nrm(ks[11], (DEPTH, ATT_HEAD_DIM), 0.05),
        "attn_sink": nrm(ks[12], (DEPTH, ATT_HEADS), 0.5),
        "w_out": nrm(ks[13], (DEPTH, MIX_WIDTH, D_MODEL), MIX_WIDTH ** -0.5),
        "norm_ffn_g": 1.0 + nrm(ks[14], (DEPTH, D_MODEL), 0.05),
        "router_w": nrm(ks[15], (DEPTH, D_MODEL, N_EXPERTS), D_MODEL ** -0.5),
        "router_b": nrm(ks[16], (DEPTH, N_EXPERTS), 0.01),
        "w_gu": nrm(ks[17], (DEPTH, N_EXPERTS, D_MODEL, 2 * D_FF), D_MODEL ** -0.5),
        "b_gu": nrm(ks[18], (DEPTH, N_EXPERTS, 2 * D_FF), 0.01),
        "w_dn": nrm(ks[19], (DEPTH, N_EXPERTS, D_FF, D_MODEL), D_FF ** -0.5),
        "b_dn": nrm(ks[20], (DEPTH, N_EXPERTS, D_MODEL), 0.01),
    }


def reference(x_prompt, x_sample, c_prompt, c_sample, ada_w, ada_b, norm_mix_g, w_in,
              ret_log_gamma_f, ret_log_gamma_b, q_norm_g, k_norm_g, attn_sink, w_out,
              norm_ffn_g, router_w, router_b, w_gu, b_gu, w_dn, b_dn):
    y_prompt = trunk(x_prompt, c_prompt, ada_w, ada_b, norm_mix_g, w_in, ret_log_gamma_f,
                     ret_log_gamma_b, q_norm_g, k_norm_g, attn_sink, w_out, norm_ffn_g,
                     router_w, router_b, w_gu, b_gu, w_dn, b_dn)
    y_sample = trunk(x_sample, c_sample, ada_w, ada_b, norm_mix_g, w_in, ret_log_gamma_f,
                     ret_log_gamma_b, q_norm_g, k_norm_g, attn_sink, w_out, norm_ffn_g,
                     router_w, router_b, w_gu, b_gu, w_dn, b_dn)
    return (y_prompt, y_sample)
```

```python
import functools

import jax
import jax.numpy as jnp
from jax import lax
from jax.experimental import pallas as pl
from jax.experimental.pallas import tpu as pltpu

F32 = jnp.float32
BF16 = jnp.bfloat16
I32 = jnp.int32

D_MODEL = 1024
RET_HEADS = 8
HEAD_DIM = 64
RET_WIDTH = RET_HEADS * HEAD_DIM
RET_CHUNK = 128
ATT_HEADS = 8
ATT_KV_HEADS = 2
ATT_WIDTH = ATT_HEADS * HEAD_DIM
ATT_KV_WIDTH = ATT_KV_HEADS * HEAD_DIM
WINDOW = 128
ATT_BLOCK = 128
IN_WIDTH = 4 * RET_WIDTH + ATT_WIDTH + 2 * ATT_KV_WIDTH
N_EXPERTS = 32
TOP_K = 4
D_FF = D_MODEL
SWIGLU_LIMIT = 7.0
SWIGLU_ALPHA = 1.702
RMS_EPS = 1e-6
NEG_INF = -1e30

LANES = 128
SUBLANES = 8
ROW_SLABS = D_MODEL // LANES
VMEM_LIMIT = 56 * 1024 * 1024


def _cparams(sem):
    return pltpu.CompilerParams(dimension_semantics=sem, vmem_limit_bytes=VMEM_LIMIT)


def _sigmoid(x):
    return 1.0 / (1.0 + jnp.exp(-x))


def _split_bf16(x):
    hi = x.astype(BF16)
    lo = (x - hi.astype(F32)).astype(BF16)
    return hi, lo


def _dot(a, b):
    return jnp.dot(a, b, preferred_element_type=F32)


def _dot_nt(a, b):
    return lax.dot_general(a, b, (((1,), (1,)), ((), ())), preferred_element_type=F32)


def _group_mean(x2, g_ref):
    hi, lo = _split_bf16(x2)
    g = g_ref[...]
    return _dot(hi, g) + _dot(lo, g)


def _ada_kernel(c_ref, w_ref, b_ref, o_ref):
    c = c_ref[...]
    a = c * _sigmoid(c)
    a_hi, a_lo = _split_bf16(a)
    w_hi, w_lo = _split_bf16(w_ref[0])
    acc = _dot(a_hi, w_hi) + _dot(a_lo, w_hi) + _dot(a_hi, w_lo)
    o_ref[0] = acc + b_ref[0]


def _ada_call(c, ada_w, ada_b):
    depth = ada_w.shape[0]
    bp = c.shape[0]
    tn = 1536
    n_out = ada_w.shape[2]
    return pl.pallas_call(
        _ada_kernel,
        grid=(depth, n_out // tn),
        in_specs=[
            pl.BlockSpec((bp, D_MODEL), lambda l, j: (0, 0)),
            pl.BlockSpec((1, D_MODEL, tn), lambda l, j: (l, 0, j)),
            pl.BlockSpec((1, 1, tn), lambda l, j: (l, 0, j)),
        ],
        out_specs=pl.BlockSpec((1, bp, tn), lambda l, j: (l, 0, j)),
        out_shape=jax.ShapeDtypeStruct((depth, bp, n_out), F32),
        compiler_params=_cparams(("arbitrary", "arbitrary")),
        name="ada_mod",
    )(c, ada_w, ada_b.reshape(depth, 1, n_out))


def _modulate(x, gain, shift, scale):
    ms = jnp.mean(x * x, axis=-1, keepdims=True)
    xn = x * lax.rsqrt(ms + RMS_EPS)
    return xn * gain * (1.0 + scale) + shift


def _in_kernel(x_ref, mod_ref, gain_ref, w_ref, g_ref, qg_ref, kg_ref,
               qr_ref, kr_ref, vr_ref, sg_ref, qa_ref, ka_ref, va_ref):
    mod = mod_ref[0]
    h = _modulate(x_ref[...], gain_ref[...], mod[:, 0:D_MODEL], mod[:, D_MODEL:2 * D_MODEL])
    proj = _dot(h.astype(BF16), w_ref[...])
    w = RET_WIDTH
    qr_ref[...] = proj[:, 0:w].astype(BF16)
    kr_ref[...] = (proj[:, w:2 * w] * (HEAD_DIM ** -0.5)).astype(BF16)
    vr_ref[...] = proj[:, 2 * w:3 * w].astype(BF16)
    gr = proj[:, 3 * w:4 * w]
    sg_ref[...] = (gr * _sigmoid(gr)).astype(BF16)
    base = 4 * w
    qg = qg_ref[...] * (HEAD_DIM ** -0.5)
    for s in range(ATT_WIDTH // LANES):
        q = proj[:, base + s * LANES: base + (s + 1) * LANES]
        msq = _group_mean(q * q, g_ref)
        qa_ref[:, s * LANES:(s + 1) * LANES] = (q * lax.rsqrt(msq + RMS_EPS) * qg).astype(BF16)
    kb = base + ATT_WIDTH
    k = proj[:, kb:kb + LANES]
    msk = _group_mean(k * k, g_ref)
    kn = k * lax.rsqrt(msk + RMS_EPS) * kg_ref[...]
    v = proj[:, kb + LANES:kb + 2 * LANES]
    lane = lax.broadcasted_iota(I32, kn.shape, 1)
    first = lane < HEAD_DIM
    for src, dst in ((kn, ka_ref), (v, va_ref)):
        rot = pltpu.roll(src, HEAD_DIM, 1)
        dst[:, 0:LANES] = jnp.where(first, src, rot).astype(BF16)
        dst[:, LANES:2 * LANES] = jnp.where(first, rot, src).astype(BF16)


def _in_call(x2, mod, gain, w_in, g128, qg, kg, seq, tm):
    t = x2.shape[0]
    row = lambda i: (i, 0)
    const = lambda i: (0, 0)
    outs = [
        jax.ShapeDtypeStruct((t, RET_WIDTH), BF16),
        jax.ShapeDtypeStruct((t, RET_WIDTH), BF16),
        jax.ShapeDtypeStruct((t, RET_WIDTH), BF16),
        jax.ShapeDtypeStruct((t, RET_WIDTH), BF16),
        jax.ShapeDtypeStruct((t, ATT_WIDTH), BF16),
        jax.ShapeDtypeStruct((t, 2 * LANES), BF16),
        jax.ShapeDtypeStruct((t, 2 * LANES), BF16),
    ]
    return pl.pallas_call(
        _in_kernel,
        grid=(t // tm,),
        in_specs=[
            pl.BlockSpec((tm, D_MODEL), row),
            pl.BlockSpec((1, 1, 6 * D_MODEL), lambda i: ((i * tm) // seq, 0, 0)),
            pl.BlockSpec((1, D_MODEL), const),
            pl.BlockSpec((D_MODEL, IN_WIDTH), const),
            pl.BlockSpec((LANES, LANES), const),
            pl.BlockSpec((1, LANES), const),
            pl.BlockSpec((1, LANES), const),
        ],
        out_specs=[pl.BlockSpec((tm, o.shape[1]), row) for o in outs],
        out_shape=outs,
        compiler_params=_cparams(("parallel",)),
        name="in_proj",
    )(x2, mod, gain, w_in, g128, qg, kg)


def _ret_kernel(lgf_ref, lgb_ref, q_ref, k_ref, v_ref, sg_ref, g_ref, o_ref,
                sf_ref, st_ref):
    p = pl.program_id(1)
    c = RET_CHUNK
    n_chunks = q_ref.shape[0] // c
    lf0, lf1 = lgf_ref[2 * p], lgf_ref[2 * p + 1]
    lb0, lb1 = lgb_ref[2 * p], lgb_ref[2 * p + 1]
    row = lax.broadcasted_iota(I32, (c, LANES), 0)
    lane = lax.broadcasted_iota(I32, (c, LANES), 1)
    pos = row.astype(F32)
    first = lane < HEAD_DIM
    lf_lane = jnp.where(first, lf0, lf1)
    lb_lane = jnp.where(first, lb0, lb1)
    lf_row = jnp.where(row < HEAD_DIM, lf0, lf1)
    lb_row = jnp.where(row < HEAD_DIM, lb0, lb1)
    qd_f = jnp.exp(lf_lane * (pos + 1.0))
    kd_f = jnp.exp(lf_lane * (c - 1.0 - pos))
    qd_b = jnp.exp(lb_lane * (c - pos))
    kd_b = jnp.exp(lb_lane * pos)
    cd_f = jnp.exp(lf_row * float(c))
    cd_b = jnp.exp(lb_row * float(c))
    same_head = (row < HEAD_DIM) == first
    row2 = lax.broadcasted_iota(I32, (c, 2 * c), 0)
    col2 = lax.broadcasted_iota(I32, (c, 2 * c), 1)
    head1 = col2 >= c
    diff = (row2 - jnp.where(head1, col2 - c, col2)).astype(F32)
    lf_c = jnp.where(head1, lf1, lf0)
    lb_c = jnp.where(head1, lb1, lb0)
    decay = jnp.where(diff >= 0, jnp.exp(lf_c * jnp.maximum(diff, 0.0)),
                      jnp.exp(lb_c * jnp.maximum(-diff, 0.0)))

    def state_update(kd, v, state, cd):
        u = _dot(kd.T.astype(BF16), v)
        return state * cd + jnp.where(same_head, u, 0.0)

    st_ref[...] = jnp.zeros_like(st_ref)

    def fwd(n, carry):
        off = pl.multiple_of(n * c, c)
        k = k_ref[pl.ds(off, c), :].astype(F32)
        v = v_ref[pl.ds(off, c), :]
        state = st_ref[...]
        sf_ref[n] = state.astype(BF16)
        st_ref[...] = state_update(k * kd_f, v, state, cd_f)
        return carry

    lax.fori_loop(0, n_chunks, fwd, 0)
    st_ref[...] = jnp.zeros_like(st_ref)

    def bwd(i, carry):
        n = n_chunks - 1 - i
        off = pl.multiple_of(n * c, c)
        qb = q_ref[pl.ds(off, c), :]
        kb = k_ref[pl.ds(off, c), :]
        vb = v_ref[pl.ds(off, c), :]
        q = qb.astype(F32)
        k = kb.astype(F32)
        zero = jnp.zeros_like(kb)
        kcat = jnp.concatenate([jnp.where(first, kb, zero), jnp.where(first, zero, kb)], axis=0)
        vcat = jnp.concatenate([jnp.where(first, vb, zero), jnp.where(first, zero, vb)], axis=0)
        s = _dot_nt(qb, kcat)
        state_b = st_ref[...]
        o = (_dot((s * decay).astype(BF16), vcat)
             + _dot((q * qd_f).astype(BF16), sf_ref[n])
             + _dot((q * qd_b).astype(BF16), state_b.astype(BF16)))
        ms = _group_mean(o * o, g_ref)
        on = o * lax.rsqrt(ms + RMS_EPS)
        o_ref[pl.ds(off, c), :] = (on * sg_ref[pl.ds(off, c), :].astype(F32)).astype(BF16)
        st_ref[...] = state_update(k * kd_b, vb, state_b, cd_b)
        return carry

    lax.fori_loop(0, n_chunks, bwd, 0)


def _ret_call(qr, kr, vr, sg, lgf, lgb, g128, batch, seq):
    t = qr.shape[0]
    blk = lambda b, p: (b, p)
    smem = pl.BlockSpec(memory_space=pltpu.SMEM)
    return pl.pallas_call(
        _ret_kernel,
        grid=(batch, RET_HEADS // 2),
        in_specs=[smem, smem] + [pl.BlockSpec((seq, LANES), blk)] * 4
                 + [pl.BlockSpec((LANES, LANES), lambda b, p: (0, 0))],
        out_specs=pl.BlockSpec((seq, LANES), blk),
        out_shape=jax.ShapeDtypeStruct((t, RET_WIDTH), BF16),
        scratch_shapes=[pltpu.VMEM((seq // RET_CHUNK, LANES, LANES), BF16),
                        pltpu.VMEM((LANES, LANES), F32)],
        compiler_params=_cparams(("parallel", "parallel")),
        name="retention",
    )(lgf, lgb, qr, kr, vr, sg, g128)


def _attn_kernel(sink_ref, q_ref, kp_ref, kc_ref, kn_ref, vp_ref, vc_ref, vn_ref,
                 bias_ref, o_ref):
    j = pl.program_id(1)
    nb = pl.num_programs(1)
    blk = ATT_BLOCK
    col = lax.broadcasted_iota(I32, (blk, 3 * blk), 1)
    outside = ((col < blk) & (j == 0)) | ((col >= 2 * blk) & (j == nb - 1))
    edge = jnp.where(outside, NEG_INF, 0.0)
    lane = lax.broadcasted_iota(I32, (3 * blk, LANES), 1)
    first = lane < HEAD_DIM
    out_first = lax.broadcasted_iota(I32, (blk, LANES), 1) < HEAD_DIM
    kk = jnp.concatenate([kp_ref[...], kc_ref[...], kn_ref[...]], axis=0)
    vv = jnp.concatenate([vp_ref[...], vc_ref[...], vn_ref[...]], axis=0)
    zero = jnp.zeros((3 * blk, LANES), BF16)
    pairs_per_kv = ATT_HEADS // ATT_KV_HEADS // 2
    for kv in range(ATT_KV_HEADS):
        kd = kk[:, kv * LANES:(kv + 1) * LANES]
        vd = vv[:, kv * LANES:(kv + 1) * LANES]
        kcat = jnp.concatenate([jnp.where(first, kd, zero), jnp.where(first, zero, kd)], axis=0)
        vcat = jnp.concatenate([jnp.where(first, vd, zero), jnp.where(first, zero, vd)], axis=0)
        for pp in range(pairs_per_kv):
            hp = kv * pairs_per_kv + pp
            qp = q_ref[:, hp * LANES:(hp + 1) * LANES]
            s = _dot_nt(qp, kcat)
            probs, inv = [], []
            for hh in range(2):
                h = 2 * hp + hh
                sh = s[:, hh * 3 * blk:(hh + 1) * 3 * blk] + bias_ref[h] + edge
                sink = sink_ref[h]
                m = jnp.maximum(jnp.max(sh, axis=-1, keepdims=True), sink)
                e = jnp.exp(sh - m)
                den = jnp.sum(e, axis=-1, keepdims=True) + jnp.exp(sink - m)
                probs.append(e.astype(BF16))
                inv.append(1.0 / den)
            o = _dot(jnp.concatenate(probs, axis=1), vcat)
            o = o * jnp.where(out_first, inv[0], inv[1])
            o_ref[:, hp * LANES:(hp + 1) * LANES] = o.astype(BF16)


def _attn_bias():
    blk = ATT_BLOCK
    i = jnp.arange(blk, dtype=F32)[:, None]
    jj = jnp.arange(3 * blk, dtype=F32)[None, :]
    rel = jnp.abs(i + blk - jj)
    slopes = 2.0 ** (-8.0 * (jnp.arange(ATT_HEADS, dtype=F32) + 1.0) / ATT_HEADS)
    return jnp.where(rel <= WINDOW, -slopes[:, None, None] * rel, NEG_INF)


def _attn_call(qa, ka, va, sink, bias, batch, seq):
    t = qa.shape[0]
    blk = ATT_BLOCK
    nb = seq // blk
    cur = lambda b, j: (b * nb + j, 0)
    prev = lambda b, j: (b * nb + jnp.maximum(j - 1, 0), 0)
    nxt = lambda b, j: (b * nb + jnp.minimum(j + 1, nb - 1), 0)
    kv_specs = [pl.BlockSpec((blk, 2 * LANES), f) for f in (prev, cur, nxt)]
    return pl.pallas_call(
        _attn_kernel,
        grid=(batch, nb),
        in_specs=[pl.BlockSpec(memory_space=pltpu.SMEM),
                  pl.BlockSpec((blk, ATT_WIDTH), cur)] + kv_specs + kv_specs
                 + [pl.BlockSpec((ATT_HEADS, blk, 3 * blk), lambda b, j: (0, 0, 0))],
        out_specs=pl.BlockSpec((blk, ATT_WIDTH), cur),
        out_shape=jax.ShapeDtypeStruct((t, ATT_WIDTH), BF16),
        compiler_params=_cparams(("parallel", "arbitrary")),
        name="window_attn",
    )(sink, qa, ka, ka, ka, va, va, va, bias)


def _route_kernel(x_ref, or_ref, oa_ref, mod_ref, wo_ref, gain_ref, rwh_ref, rwl_ref,
                  rb_ref, tri_ref, xo_ref, hf_ref, topi_ref, gate_ref, rank_ref, cnt_ref):
    tm = x_ref.shape[0]
    mod = mod_ref[0]
    d = D_MODEL
    mix = _dot(or_ref[...], wo_ref[0:RET_WIDTH, :]) + _dot(oa_ref[...], wo_ref[RET_WIDTH:, :])
    x1 = x_ref[...] + mod[:, 2 * d:3 * d] * mix
    xo_ref[...] = x1
    hf = _modulate(x1, gain_ref[...], mod[:, 3 * d:4 * d], mod[:, 4 * d:5 * d])
    for c in range(ROW_SLABS):
        hf_ref[pl.ds(c, tm, stride=ROW_SLABS), :] = hf[:, c * LANES:(c + 1) * LANES]
    h_hi, h_lo = _split_bf16(hf)
    rwh = rwh_ref[...]
    logits = _dot(h_hi, rwh) + _dot(h_lo, rwh) + _dot(h_hi, rwl_ref[...]) + rb_ref[...]
    lane = lax.broadcasted_iota(I32, logits.shape, 1)
    work = logits
    vals, hots = [], []
    topi = jnp.zeros(logits.shape, I32)
    for k in range(TOP_K):
        m = jnp.max(work, axis=-1, keepdims=True)
        idx = jnp.min(jnp.where(work == m, lane, LANES), axis=-1, keepdims=True)
        hot = lane == idx
        work = jnp.where(hot, -jnp.inf, work)
        vals.append(m)
        hots.append(hot)
        topi = jnp.where(lane == k, idx, topi)
    exps = [jnp.exp(v - vals[0]) for v in vals]
    den = exps[0] + exps[1] + exps[2] + exps[3]
    sel = hots[0] | hots[1] | hots[2] | hots[3]
    sel_f = jnp.where(sel, 1.0, 0.0)
    before = _dot(tri_ref[...], sel_f.astype(BF16))
    gates = jnp.zeros(logits.shape, F32)
    ranks = jnp.zeros(logits.shape, F32)
    for k in range(TOP_K):
        gates = jnp.where(lane == k, exps[k] / den, gates)
        rk = jnp.sum(jnp.where(hots[k], before, 0.0), axis=-1, keepdims=True)
        ranks = jnp.where(lane == k, rk, ranks)
    topi_ref[...] = topi
    gate_ref[...] = gates
    rank_ref[...] = ranks.astype(I32)
    cnt_ref[0] = jnp.sum(sel_f, axis=0, keepdims=True).astype(I32)


def _route_call(x2, o_r, o_a, mod, w_out, gain, rw_hi, rw_lo, rb, tri, seq, tm):
    t = x2.shape[0]
    nt = t // tm
    row = lambda i: (i, 0)
    const = lambda i: (0, 0)
    outs = [
        jax.ShapeDtypeStruct((t, D_MODEL), F32),
        jax.ShapeDtypeStruct((t * ROW_SLABS, LANES), F32),
        jax.ShapeDtypeStruct((t, LANES), I32),
        jax.ShapeDtypeStruct((t, LANES), F32),
        jax.ShapeDtypeStruct((t, LANES), I32),
        jax.ShapeDtypeStruct((nt, 1, LANES), I32),
    ]
    out_specs = [
        pl.BlockSpec((tm, D_MODEL), row),
        pl.BlockSpec((tm * ROW_SLABS, LANES), row),
        pl.BlockSpec((tm, LANES), row),
        pl.BlockSpec((tm, LANES), row),
        pl.BlockSpec((tm, LANES), row),
        pl.BlockSpec((1, 1, LANES), lambda i: (i, 0, 0)),
    ]
    return pl.pallas_call(
        _route_kernel,
        grid=(nt,),
        in_specs=[
            pl.BlockSpec((tm, D_MODEL), row),
            pl.BlockSpec((tm, RET_WIDTH), row),
            pl.BlockSpec((tm, ATT_WIDTH), row),
            pl.BlockSpec((1, 1, 6 * D_MODEL), lambda i: ((i * tm) // seq, 0, 0)),
            pl.BlockSpec((D_MODEL, D_MODEL), const),
            pl.BlockSpec((1, D_MODEL), const),
            pl.BlockSpec((D_MODEL, LANES), const),
            pl.BlockSpec((D_MODEL, LANES), const),
            pl.BlockSpec((1, LANES), const),
            pl.BlockSpec((tm, tm), const),
        ],
        out_specs=out_specs,
        out_shape=outs,
        compiler_params=_cparams(("parallel",)),
        name="out_proj_route",
    )(x2, o_r, o_a, mod, w_out, gain, rw_hi, rw_lo, rb, tri)


def _row_slab(ref, row8):
    return ref.at[pl.ds(pl.multiple_of(row8, SUBLANES), SUBLANES), :]


def _dispatch_kernel(dest_ref, src_ref, xs_in_ref, xs_ref, sem):
    del xs_in_ref
    tm = src_ref.shape[0] // ROW_SLABS

    def issue(t, carry):
        for k in range(TOP_K):
            pltpu.make_async_copy(_row_slab(src_ref, t * ROW_SLABS),
                                  _row_slab(xs_ref, dest_ref[k, t]), sem).start()
        return carry

    lax.fori_loop(0, tm, issue, 0)
    for k in range(TOP_K):
        pltpu.make_async_copy(src_ref, xs_ref.at[pl.ds(0, tm * ROW_SLABS), :], sem).wait()


def _dispatch_call(dest8, hf_tok, xs_prev, tm):
    t = hf_tok.shape[0] // ROW_SLABS
    return pl.pallas_call(
        _dispatch_kernel,
        grid=(t // tm,),
        in_specs=[
            pl.BlockSpec((TOP_K, tm), lambda i: (0, i), memory_space=pltpu.SMEM),
            pl.BlockSpec((tm * ROW_SLABS, LANES), lambda i: (i, 0)),
            pl.BlockSpec(memory_space=pl.ANY),
        ],
        out_specs=pl.BlockSpec(memory_space=pl.ANY),
        out_shape=jax.ShapeDtypeStruct(xs_prev.shape, F32),
        scratch_shapes=[pltpu.SemaphoreType.DMA(())],
        input_output_aliases={2: 0},
        compiler_params=_cparams(("arbitrary",)),
        name="moe_dispatch",
    )(dest8, hf_tok, xs_prev)


def _moe_kernel(be_ref, nu_ref, xs_ref, wg_ref, wl_ref, bg_ref, bl_ref, wd_ref, bd_ref,
                y_ref, xb_ref):
    del be_ref
    i = pl.program_id(0)
    tm = xb_ref.shape[0]

    @pl.when(i < nu_ref[0])
    def _():
        for c in range(ROW_SLABS):
            xb_ref[:, c * LANES:(c + 1) * LANES] = (
                xs_ref[pl.ds(c, tm, stride=ROW_SLABS), :].astype(BF16))
        x = xb_ref[...]
        g = _dot(x, wg_ref[0]) + bg_ref[0]
        lin = _dot(x, wl_ref[0]) + bl_ref[0]
        glu = jnp.minimum(g, SWIGLU_LIMIT)
        lin = jnp.clip(lin, -SWIGLU_LIMIT, SWIGLU_LIMIT)
        act = (lin + 1.0) * glu * _sigmoid(SWIGLU_ALPHA * glu)
        y = _dot(act.astype(BF16), wd_ref[0]) + bd_ref[0]
        for c in range(ROW_SLABS):
            y_ref[pl.ds(c, tm, stride=ROW_SLABS), :] = y[:, c * LANES:(c + 1) * LANES]

    @pl.when(i >= nu_ref[0])
    def _():
        y_ref[...] = jnp.zeros_like(y_ref)


def _moe_call(block_e, n_used, xs, wg, wl, bg, bl, wd, bd, tm):
    n_tiles = xs.shape[0] // (tm * ROW_SLABS)
    row = lambda i, be, nu: (i, 0)
    wsel = lambda i, be, nu: (be[i], 0, 0)
    grid_spec = pltpu.PrefetchScalarGridSpec(
        num_scalar_prefetch=2,
        grid=(n_tiles,),
        in_specs=[
            pl.BlockSpec((tm * ROW_SLABS, LANES), row),
            pl.BlockSpec((1, D_MODEL, D_FF), wsel),
            pl.BlockSpec((1, D_MODEL, D_FF), wsel),
            pl.BlockSpec((1, 1, D_FF), wsel),
            pl.BlockSpec((1, 1, D_FF), wsel),
            pl.BlockSpec((1, D_FF, D_MODEL), wsel),
            pl.BlockSpec((1, 1, D_MODEL), wsel),
        ],
        out_specs=pl.BlockSpec((tm * ROW_SLABS, LANES), row),
        scratch_shapes=[pltpu.VMEM((tm, D_MODEL), BF16)],
    )
    return pl.pallas_call(
        _moe_kernel,
        grid_spec=grid_spec,
        out_shape=jax.ShapeDtypeStruct(xs.shape, F32),
        compiler_params=_cparams(("arbitrary",)),
        name="moe_experts",
    )(block_e, n_used, xs, wg, wl, bg, bl, wd, bd)


def _combine_kernel(dest_ref, x_ref, gate_ref, mod_ref, y_ref, o_ref, buf_ref, sem):
    tm = x_ref.shape[0]

    def issue(t, carry):
        for k in range(TOP_K):
            pltpu.make_async_copy(_row_slab(y_ref, dest_ref[k, t]),
                                  _row_slab(buf_ref.at[k], t * ROW_SLABS), sem).start()
        return carry

    lax.fori_loop(0, tm, issue, 0)
    for k in range(TOP_K):
        pltpu.make_async_copy(y_ref.at[pl.ds(0, tm * ROW_SLABS), :], buf_ref.at[k], sem).wait()
    mod = mod_ref[0]
    gates = gate_ref[...]
    for c in range(ROW_SLABS):
        acc = jnp.zeros((tm, LANES), F32)
        for k in range(TOP_K):
            acc = acc + gates[:, k:k + 1] * buf_ref[k, pl.ds(c, tm, stride=ROW_SLABS), :]
        sl = slice(c * LANES, (c + 1) * LANES)
        o_ref[:, sl] = x_ref[:, sl] + mod[:, 5 * D_MODEL + c * LANES:5 * D_MODEL + (c + 1) * LANES] * acc


def _combine_call(dest8, x2, gates, mod, y, seq, tm):
    t = x2.shape[0]
    row = lambda i: (i, 0)
    return pl.pallas_call(
        _combine_kernel,
        grid=(t // tm,),
        in_specs=[
            pl.BlockSpec((TOP_K, tm), lambda i: (0, i), memory_space=pltpu.SMEM),
            pl.BlockSpec((tm, D_MODEL), row),
            pl.BlockSpec((tm, LANES), row),
            pl.BlockSpec((1, 1, 6 * D_MODEL), lambda i: ((i * tm) // seq, 0, 0)),
            pl.BlockSpec(memory_space=pl.ANY),
        ],
        out_specs=pl.BlockSpec((tm, D_MODEL), row),
        out_shape=jax.ShapeDtypeStruct((t, D_MODEL), F32),
        scratch_shapes=[pltpu.VMEM((TOP_K, tm * ROW_SLABS, LANES), F32),
                        pltpu.SemaphoreType.DMA(())],
        compiler_params=_cparams(("arbitrary",)),
        name="moe_combine",
    )(dest8, x2, gates, mod, y)


def _plan(topi, rank, cnt, tm_route, tm_moe, n_tiles_moe):
    t = topi.shape[0]
    cnt = cnt[:, 0, :N_EXPERTS]
    totals = jnp.sum(cnt, axis=0)
    padded = (totals + tm_moe - 1) // tm_moe * tm_moe
    pad_end = jnp.cumsum(padded)
    pad_start = pad_end - padded
    tile_base = pad_start[None, :] + jnp.cumsum(cnt, axis=0) - cnt
    top = topi[:, :TOP_K]
    base_tok = jnp.repeat(tile_base, tm_route, axis=0)
    hot = top[:, :, None] == jnp.arange(N_EXPERTS, dtype=I32)[None, None, :]
    base = jnp.sum(jnp.where(hot, base_tok[:, None, :], 0), axis=-1)
    dest = base + rank[:, :TOP_K]
    dest8 = (dest * ROW_SLABS).T.astype(I32)
    tile_start = jnp.arange(n_tiles_moe, dtype=I32) * tm_moe
    block_e = jnp.minimum(jnp.sum(tile_start[:, None] >= pad_end[None, :], axis=1),
                          N_EXPERTS - 1).astype(I32)
    n_used = (pad_end[-1] // tm_moe).astype(I32).reshape(1)
    return dest8, block_e, n_used


def _pick(n, pref):
    for c in pref:
        if n % c == 0:
            return c
    raise ValueError(f"no tile size in {pref} divides {n}")


def _trunk(x, mods, prm):
    batch, seq, _ = x.shape
    t = batch * seq
    assert seq % RET_CHUNK == 0 and seq % ATT_BLOCK == 0
    tm_in = _pick(seq, (512, 256, 128))
    tm_route = _pick(seq, (512, 256, 128))
    tm_disp = _pick(t, (512, 256, 128))
    tm_comb = _pick(seq, (256, 128))
    tm_moe = 512 if t * TOP_K >= 65536 * 4 else (256 if t * TOP_K >= 16384 else 128)
    n_tiles_moe = (t * TOP_K) // tm_moe + N_EXPERTS
    x2 = x.reshape(t, D_MODEL)
    xs = jnp.zeros((n_tiles_moe * tm_moe * ROW_SLABS, LANES), F32)
    tri = (jnp.arange(tm_route)[:, None] > jnp.arange(tm_route)[None, :]).astype(BF16)
    depth = prm["w_in"].shape[0]
    for l in range(depth):
        mod = mods[l]
        qr, kr, vr, sg, qa, ka, va = _in_call(
            x2, mod, prm["norm_mix_g"][l], prm["w_in"][l], prm["g128"],
            prm["q_g"][l], prm["k_g"][l], seq, tm_in)
        o_r = _ret_call(qr, kr, vr, sg, prm["lgf"][l], prm["lgb"][l], prm["g128"], batch, seq)
        o_a = _attn_call(qa, ka, va, prm["sink"][l], prm["bias"], batch, seq)
        x2, hf_tok, topi, gates, rank, cnt = _route_call(
            x2, o_r, o_a, mod, prm["w_out"][l], prm["norm_ffn_g"][l],
            prm["rw_hi"][l], prm["rw_lo"][l], prm["rb"][l], tri, seq, tm_route)
        dest8, block_e, n_used = _plan(topi, rank, cnt, tm_route, tm_moe, n_tiles_moe)
        xs = _dispatch_call(dest8, hf_tok, xs, tm_disp)
        y = _moe_call(block_e, n_used, xs, prm["w_g"][l], prm["w_l"][l], prm["b_g"][l],
                      prm["b_l"][l], prm["w_dn"][l], prm["b_dn"][l], tm_moe)
        x2 = _combine_call(dest8, x2, gates, mod, y, seq, tm_comb)
    return x2.reshape(batch, seq, D_MODEL)


def _prepare(ada_w, ada_b, norm_mix_g, w_in, ret_log_gamma_f, ret_log_gamma_b, q_norm_g,
             k_norm_g, attn_sink, w_out, norm_ffn_g, router_w, router_b, w_gu, b_gu, w_dn, b_dn):
    depth = w_in.shape[0]
    pad_e = LANES - N_EXPERTS
    rw = jnp.pad(router_w, ((0, 0), (0, 0), (0, pad_e)))
    rw_hi = rw.astype(BF16)
    rw_lo = (rw - rw_hi.astype(F32)).astype(BF16)
    rb = jnp.pad(router_b, ((0, 0), (0, pad_e)), constant_values=NEG_INF).reshape(depth, 1, LANES)
    lane = jnp.arange(LANES)
    g128 = jnp.where((lane[:, None] // HEAD_DIM) == (lane[None, :] // HEAD_DIM),
                     1.0 / HEAD_DIM, 0.0).astype(BF16)
    return dict(
        norm_mix_g=norm_mix_g.reshape(depth, 1, D_MODEL),
        norm_ffn_g=norm_ffn_g.reshape(depth, 1, D_MODEL),
        w_in=w_in.astype(BF16),
        w_out=w_out.astype(BF16),
        lgf=ret_log_gamma_f.astype(F32),
        lgb=ret_log_gamma_b.astype(F32),
        q_g=jnp.tile(q_norm_g, (1, LANES // HEAD_DIM)).reshape(depth, 1, LANES),
        k_g=jnp.tile(k_norm_g, (1, LANES // HEAD_DIM)).reshape(depth, 1, LANES),
        sink=attn_sink.astype(F32),
        rw_hi=rw_hi, rw_lo=rw_lo, rb=rb,
        w_g=w_gu[..., 0::2].astype(BF16),
        w_l=w_gu[..., 1::2].astype(BF16),
        b_g=b_gu[..., 0::2].reshape(depth, N_EXPERTS, 1, D_FF),
        b_l=b_gu[..., 1::2].reshape(depth, N_EXPERTS, 1, D_FF),
        w_dn=w_dn.astype(BF16),
        b_dn=b_dn.reshape(depth, N_EXPERTS, 1, D_MODEL),
        g128=g128,
        bias=_attn_bias(),
    )


def kernel(x_prompt, x_sample, c_prompt, c_sample, ada_w, ada_b, norm_mix_g, w_in, ret_log_gamma_f, ret_log_gamma_b, q_norm_g, k_norm_g, attn_sink, w_out, norm_ffn_g, router_w, router_b, w_gu, b_gu, w_dn, b_dn):
    prm = _prepare(ada_w, ada_b, norm_mix_g, w_in, ret_log_gamma_f, ret_log_gamma_b, q_norm_g,
                   k_norm_g, attn_sink, w_out, norm_ffn_g, router_w, router_b, w_gu, b_gu,
                   w_dn, b_dn)
    bp, bs = c_prompt.shape[0], c_sample.shape[0]
    n_c = bp + bs
    n_pad = -n_c % SUBLANES
    c_all = jnp.concatenate([c_prompt, c_sample, jnp.zeros((n_pad, D_MODEL), F32)], axis=0)
    mods = _ada_call(c_all, ada_w, ada_b)
    depth = ada_w.shape[0]
    mods_p = mods[:, :bp].reshape(depth, bp, 1, 6 * D_MODEL)
    mods_s = mods[:, bp:n_c].reshape(depth, bs, 1, 6 * D_MODEL)
    y_prompt = _trunk(x_prompt, mods_p, prm)
    y_sample = _trunk(x_sample, mods_s, prm)
    return (y_prompt, y_sample)
```

```python
import functools

import jax
import jax.numpy as jnp
from jax import lax
from jax.experimental import pallas as pl
from jax.experimental.pallas import tpu as pltpu

F32 = jnp.float32
BF16 = jnp.bfloat16
I32 = jnp.int32

D_MODEL = 1024
RET_HEADS = 8
HEAD_DIM = 64
RET_WIDTH = RET_HEADS * HEAD_DIM
RET_CHUNK = 128
ATT_HEADS = 8
ATT_KV_HEADS = 2
ATT_WIDTH = ATT_HEADS * HEAD_DIM
ATT_KV_WIDTH = ATT_KV_HEADS * HEAD_DIM
WINDOW = 128
ATT_BLOCK = 128
IN_WIDTH = 4 * RET_WIDTH + ATT_WIDTH + 2 * ATT_KV_WIDTH
N_EXPERTS = 32
TOP_K = 4
D_FF = D_MODEL
SWIGLU_LIMIT = 7.0
SWIGLU_ALPHA = 1.702
RMS_EPS = 1e-6
NEG_INF = -1e30

LANES = 128
SUBLANES = 8
ROW_SLABS = D_MODEL // LANES
VMEM_LIMIT = 56 * 1024 * 1024


def _cparams(sem):
    return pltpu.CompilerParams(dimension_semantics=sem, vmem_limit_bytes=VMEM_LIMIT)


def _sigmoid(x):
    return 1.0 / (1.0 + jnp.exp(-x))


def _split_bf16(x):
    hi = x.astype(BF16)
    lo = (x - hi.astype(F32)).astype(BF16)
    return hi, lo


def _dot(a, b):
    return jnp.dot(a, b, preferred_element_type=F32)


def _dot_nt(a, b):
    return lax.dot_general(a, b, (((1,), (1,)), ((), ())), preferred_element_type=F32)


def _group_mean(x2, g_ref):
    hi, lo = _split_bf16(x2)
    g = g_ref[...]
    return _dot(hi, g) + _dot(lo, g)


def _ada_kernel(c_ref, w_ref, b_ref, o_ref):
    c = c_ref[...]
    a = c * _sigmoid(c)
    a_hi, a_lo = _split_bf16(a)
    w_hi, w_lo = _split_bf16(w_ref[0])
    acc = _dot(a_hi, w_hi) + _dot(a_lo, w_hi) + _dot(a_hi, w_lo)
    o_ref[0] = acc + b_ref[0]


def _ada_call(c, ada_w, ada_b):
    depth = ada_w.shape[0]
    bp = c.shape[0]
    tn = 1536
    n_out = ada_w.shape[2]
    return pl.pallas_call(
        _ada_kernel,
        grid=(depth, n_out // tn),
        in_specs=[
            pl.BlockSpec((bp, D_MODEL), lambda l, j: (0, 0)),
            pl.BlockSpec((1, D_MODEL, tn), lambda l, j: (l, 0, j)),
            pl.BlockSpec((1, 1, tn), lambda l, j: (l, 0, j)),
        ],
        out_specs=pl.BlockSpec((1, bp, tn), lambda l, j: (l, 0, j)),
        out_shape=jax.ShapeDtypeStruct((depth, bp, n_out), F32),
        compiler_params=_cparams(("arbitrary", "arbitrary")),
        name="ada_mod",
    )(c, ada_w, ada_b.reshape(depth, 1, n_out))


DEINT_COLS = 256


def _deint_kernel(w_ref, p_ref, g_ref, l_ref):
    perm = p_ref[...]
    half = DEINT_COLS // 2
    for j in range(w_ref.shape[2] // DEINT_COLS):
        w = w_ref[0, :, j * DEINT_COLS:(j + 1) * DEINT_COLS].astype(BF16)
        r = _dot(w, perm)
        g_ref[0, :, j * half:(j + 1) * half] = r[:, :half].astype(BF16)
        l_ref[0, :, j * half:(j + 1) * half] = r[:, half:].astype(BF16)


def _deint_call(w_gu):
    depth, n_e, d_in, d_out2 = w_gu.shape
    w = w_gu.reshape(depth * n_e, d_in, d_out2)
    tk = 512
    src = jnp.arange(DEINT_COLS)
    dst = jnp.where(src % 2 == 0, src // 2, DEINT_COLS // 2 + src // 2)
    perm = (dst[:, None] == jnp.arange(DEINT_COLS)[None, :]).astype(BF16)
    out = jax.ShapeDtypeStruct((depth * n_e, d_in, d_out2 // 2), BF16)
    blk = lambda e, i: (e, i, 0)
    w_g, w_l = pl.pallas_call(
        _deint_kernel,
        grid=(depth * n_e, d_in // tk),
        in_specs=[pl.BlockSpec((1, tk, d_out2), blk),
                  pl.BlockSpec((DEINT_COLS, DEINT_COLS), lambda e, i: (0, 0))],
        out_specs=[pl.BlockSpec((1, tk, d_out2 // 2), blk)] * 2,
        out_shape=[out, out],
        compiler_params=_cparams(("parallel", "parallel")),
        name="expert_weight_split",
    )(w, perm)
    shape = (depth, n_e, d_in, d_out2 // 2)
    return w_g.reshape(shape), w_l.reshape(shape)


def _modulate(x, gain, shift, scale):
    ms = jnp.mean(x * x, axis=-1, keepdims=True)
    xn = x * lax.rsqrt(ms + RMS_EPS)
    return xn * gain * (1.0 + scale) + shift


def _in_kernel(x_ref, mod_ref, gain_ref, w_ref, g_ref, qg_ref, kg_ref,
               qr_ref, kr_ref, vr_ref, sg_ref, qa_ref, ka_ref, va_ref):
    mod = mod_ref[0]
    h = _modulate(x_ref[...], gain_ref[...], mod[:, 0:D_MODEL], mod[:, D_MODEL:2 * D_MODEL])
    proj = _dot(h.astype(BF16), w_ref[...])
    w = RET_WIDTH
    qr_ref[...] = proj[:, 0:w].astype(BF16)
    kr_ref[...] = (proj[:, w:2 * w] * (HEAD_DIM ** -0.5)).astype(BF16)
    vr_ref[...] = proj[:, 2 * w:3 * w].astype(BF16)
    gr = proj[:, 3 * w:4 * w]
    sg_ref[...] = (gr * _sigmoid(gr)).astype(BF16)
    base = 4 * w
    qg = qg_ref[...] * (HEAD_DIM ** -0.5)
    for s in range(ATT_WIDTH // LANES):
        q = proj[:, base + s * LANES: base + (s + 1) * LANES]
        msq = _group_mean(q * q, g_ref)
        qa_ref[:, s * LANES:(s + 1) * LANES] = (q * lax.rsqrt(msq + RMS_EPS) * qg).astype(BF16)
    kb = base + ATT_WIDTH
    k = proj[:, kb:kb + LANES]
    msk = _group_mean(k * k, g_ref)
    kn = k * lax.rsqrt(msk + RMS_EPS) * kg_ref[...]
    v = proj[:, kb + LANES:kb + 2 * LANES]
    lane = lax.broadcasted_iota(I32, kn.shape, 1)
    first = lane < HEAD_DIM
    for src, dst in ((kn, ka_ref), (v, va_ref)):
        rot = pltpu.roll(src, HEAD_DIM, 1)
        dst[:, 0:LANES] = jnp.where(first, src, rot).astype(BF16)
        dst[:, LANES:2 * LANES] = jnp.where(first, rot, src).astype(BF16)


def _in_call(x2, mod, gain, w_in, g128, qg, kg, seq, tm):
    t = x2.shape[0]
    row = lambda i: (i, 0)
    const = lambda i: (0, 0)
    outs = [
        jax.ShapeDtypeStruct((t, RET_WIDTH), BF16),
        jax.ShapeDtypeStruct((t, RET_WIDTH), BF16),
        jax.ShapeDtypeStruct((t, RET_WIDTH), BF16),
        jax.ShapeDtypeStruct((t, RET_WIDTH), BF16),
        jax.ShapeDtypeStruct((t, ATT_WIDTH), BF16),
        jax.ShapeDtypeStruct((t, 2 * LANES), BF16),
        jax.ShapeDtypeStruct((t, 2 * LANES), BF16),
    ]
    return pl.pallas_call(
        _in_kernel,
        grid=(t // tm,),
        in_specs=[
            pl.BlockSpec((tm, D_MODEL), row),
            pl.BlockSpec((1, 1, 6 * D_MODEL), lambda i: ((i * tm) // seq, 0, 0)),
            pl.BlockSpec((1, D_MODEL), const),
            pl.BlockSpec((D_MODEL, IN_WIDTH), const),
            pl.BlockSpec((LANES, LANES), const),
            pl.BlockSpec((1, LANES), const),
            pl.BlockSpec((1, LANES), const),
        ],
        out_specs=[pl.BlockSpec((tm, o.shape[1]), row) for o in outs],
        out_shape=outs,
        compiler_params=_cparams(("parallel",)),
        name="in_proj",
    )(x2, mod, gain, w_in, g128, qg, kg)


def _ret_kernel(lgf_ref, lgb_ref, q_ref, k_ref, v_ref, sg_ref, g_ref, o_ref,
                uf_ref, ub_ref, sf_ref, sb_ref, s_ref, acc_ref):
    p = pl.program_id(1)
    c = RET_CHUNK
    n_chunks = q_ref.shape[0] // c
    lf0, lf1 = lgf_ref[2 * p], lgf_ref[2 * p + 1]
    lb0, lb1 = lgb_ref[2 * p], lgb_ref[2 * p + 1]
    row = lax.broadcasted_iota(I32, (c, LANES), 0)
    lane = lax.broadcasted_iota(I32, (c, LANES), 1)
    pos = row.astype(F32)
    first = lane < HEAD_DIM
    lf_lane = jnp.where(first, lf0, lf1)
    lb_lane = jnp.where(first, lb0, lb1)
    qd_f = jnp.exp(lf_lane * (pos + 1.0))
    kd_f = jnp.exp(lf_lane * (c - 1.0 - pos))
    qd_b = jnp.exp(lb_lane * (c - pos))
    kd_b = jnp.exp(lb_lane * pos)
    cd_f = jnp.exp(lf_lane * float(c))
    cd_b = jnp.exp(lb_lane * float(c))
    same_head = (row < HEAD_DIM) == first
    row2 = lax.broadcasted_iota(I32, (c, 2 * c), 0)
    col2 = lax.broadcasted_iota(I32, (c, 2 * c), 1)
    head1 = col2 >= c
    diff = (row2 - jnp.where(head1, col2 - c, col2)).astype(F32)
    lf_c = jnp.where(head1, lf1, lf0)
    lb_c = jnp.where(head1, lb1, lb0)
    decay = jnp.where(diff >= 0, jnp.exp(lf_c * jnp.maximum(diff, 0.0)),
                      jnp.exp(lb_c * jnp.maximum(-diff, 0.0)))

    unroll = _pick(n_chunks, (4, 2, 1))

    def updates(g, carry):
        for u in range(unroll):
            n = g * unroll + u
            off = pl.multiple_of(n * c, c)
            k = k_ref[pl.ds(off, c), :].astype(F32)
            vt = v_ref[pl.ds(off, c), :].astype(F32).T.astype(BF16)
            uf_ref[n] = jnp.where(same_head, _dot(vt, (k * kd_f).astype(BF16)), 0.0)
            ub_ref[n] = jnp.where(same_head, _dot(vt, (k * kd_b).astype(BF16)), 0.0)
        return carry

    lax.fori_loop(0, n_chunks // unroll, updates, 0)

    def scan_f(n, state):
        sf_ref[n] = state.astype(BF16)
        return state * cd_f + uf_ref[n]

    def scan_b(i, state):
        n = n_chunks - 1 - i
        sb_ref[n] = state.astype(BF16)
        return state * cd_b + ub_ref[n]

    zero_state = jnp.zeros((LANES, LANES), F32)
    lax.fori_loop(0, n_chunks, scan_f, zero_state)
    lax.fori_loop(0, n_chunks, scan_b, zero_state)

    group = s_ref.shape[0]
    zero = jnp.zeros((c, LANES), BF16)

    def split_heads(x):
        return jnp.concatenate([jnp.where(first, x, zero), jnp.where(first, zero, x)], axis=0)

    def outputs(g, carry):
        offs = [pl.multiple_of((g * group + u) * c, c) for u in range(group)]
        for u in range(group):
            s_ref[u] = _dot_nt(q_ref[pl.ds(offs[u], c), :],
                               split_heads(k_ref[pl.ds(offs[u], c), :]))
        for u in range(group):
            n = g * group + u
            q = q_ref[pl.ds(offs[u], c), :].astype(F32)
            acc_ref[u] = (_dot((s_ref[u] * decay).astype(BF16), split_heads(v_ref[pl.ds(offs[u], c), :]))
                          + _dot_nt((q * qd_f).astype(BF16), sf_ref[n])
                          + _dot_nt((q * qd_b).astype(BF16), sb_ref[n]))
        for u in range(group):
            o = acc_ref[u]
            ms = _group_mean(o * o, g_ref)
            on = o * lax.rsqrt(ms + RMS_EPS)
            o_ref[pl.ds(offs[u], c), :] = (on * sg_ref[pl.ds(offs[u], c), :].astype(F32)).astype(BF16)
        return carry

    lax.fori_loop(0, n_chunks // group, outputs, 0)


def _ret_call(qr, kr, vr, sg, lgf, lgb, g128, batch, seq):
    t = qr.shape[0]
    blk = lambda b, p: (b, p)
    smem = pl.BlockSpec(memory_space=pltpu.SMEM)
    n_chunks = seq // RET_CHUNK
    group = _pick(n_chunks, (8, 4, 2, 1))
    return pl.pallas_call(
        _ret_kernel,
        grid=(batch, RET_HEADS // 2),
        in_specs=[smem, smem] + [pl.BlockSpec((seq, LANES), blk)] * 4
                 + [pl.BlockSpec((LANES, LANES), lambda b, p: (0, 0))],
        out_specs=pl.BlockSpec((seq, LANES), blk),
        out_shape=jax.ShapeDtypeStruct((t, RET_WIDTH), BF16),
        scratch_shapes=[pltpu.VMEM((n_chunks, LANES, LANES), F32),
                        pltpu.VMEM((n_chunks, LANES, LANES), F32),
                        pltpu.VMEM((n_chunks, LANES, LANES), BF16),
                        pltpu.VMEM((n_chunks, LANES, LANES), BF16),
                        pltpu.VMEM((group, RET_CHUNK, 2 * RET_CHUNK), F32),
                        pltpu.VMEM((group, RET_CHUNK, LANES), F32)],
        compiler_params=_cparams(("parallel", "parallel")),
        name="retention",
    )(lgf, lgb, qr, kr, vr, sg, g128)


def _attn_kernel(sink_ref, q_ref, kp_ref, kc_ref, kn_ref, vp_ref, vc_ref, vn_ref,
                 bias_ref, o_ref):
    j = pl.program_id(1)
    nb = pl.num_programs(1)
    blk = ATT_BLOCK
    col = lax.broadcasted_iota(I32, (blk, 3 * blk), 1)
    outside = ((col < blk) & (j == 0)) | ((col >= 2 * blk) & (j == nb - 1))
    edge = jnp.where(outside, NEG_INF, 0.0)
    lane = lax.broadcasted_iota(I32, (3 * blk, LANES), 1)
    first = lane < HEAD_DIM
    out_first = lax.broadcasted_iota(I32, (blk, LANES), 1) < HEAD_DIM
    kk = jnp.concatenate([kp_ref[...], kc_ref[...], kn_ref[...]], axis=0)
    vv = jnp.concatenate([vp_ref[...], vc_ref[...], vn_ref[...]], axis=0)
    zero = jnp.zeros((3 * blk, LANES), BF16)
    pairs_per_kv = ATT_HEADS // ATT_KV_HEADS // 2
    for kv in range(ATT_KV_HEADS):
        kd = kk[:, kv * LANES:(kv + 1) * LANES]
        vd = vv[:, kv * LANES:(kv + 1) * LANES]
        kcat = jnp.concatenate([jnp.where(first, kd, zero), jnp.where(first, zero, kd)], axis=0)
        vcat = jnp.concatenate([jnp.where(first, vd, zero), jnp.where(first, zero, vd)], axis=0)
        for pp in range(pairs_per_kv):
            hp = kv * pairs_per_kv + pp
            qp = q_ref[:, hp * LANES:(hp + 1) * LANES]
            s = _dot_nt(qp, kcat)
            probs, inv = [], []
            for hh in range(2):
                h = 2 * hp + hh
                sh = s[:, hh * 3 * blk:(hh + 1) * 3 * blk] + bias_ref[h] + edge
                sink = sink_ref[h]
                m = jnp.maximum(jnp.max(sh, axis=-1, keepdims=True), sink)
                e = jnp.exp(sh - m)
                den = jnp.sum(e, axis=-1, keepdims=True) + jnp.exp(sink - m)
                probs.append(e.astype(BF16))
                inv.append(1.0 / den)
            o = _dot(jnp.concatenate(probs, axis=1), vcat)
            o = o * jnp.where(out_first, inv[0], inv[1])
            o_ref[:, hp * LANES:(hp + 1) * LANES] = o.astype(BF16)


def _attn_bias():
    blk = ATT_BLOCK
    i = jnp.arange(blk, dtype=F32)[:, None]
    jj = jnp.arange(3 * blk, dtype=F32)[None, :]
    rel = jnp.abs(i + blk - jj)
    slopes = 2.0 ** (-8.0 * (jnp.arange(ATT_HEADS, dtype=F32) + 1.0) / ATT_HEADS)
    return jnp.where(rel <= WINDOW, -slopes[:, None, None] * rel, NEG_INF)


def _attn_call(qa, ka, va, sink, bias, batch, seq):
    t = qa.shape[0]
    blk = ATT_BLOCK
    nb = seq // blk
    cur = lambda b, j: (b * nb + j, 0)
    prev = lambda b, j: (b * nb + jnp.maximum(j - 1, 0), 0)
    nxt = lambda b, j: (b * nb + jnp.minimum(j + 1, nb - 1), 0)
    kv_specs = [pl.BlockSpec((blk, 2 * LANES), f) for f in (prev, cur, nxt)]
    return pl.pallas_call(
        _attn_kernel,
        grid=(batch, nb),
        in_specs=[pl.BlockSpec(memory_space=pltpu.SMEM),
                  pl.BlockSpec((blk, ATT_WIDTH), cur)] + kv_specs + kv_specs
                 + [pl.BlockSpec((ATT_HEADS, blk, 3 * blk), lambda b, j: (0, 0, 0))],
        out_specs=pl.BlockSpec((blk, ATT_WIDTH), cur),
        out_shape=jax.ShapeDtypeStruct((t, ATT_WIDTH), BF16),
        compiler_params=_cparams(("parallel", "arbitrary")),
        name="window_attn",
    )(sink, qa, ka, ka, ka, va, va, va, bias)


def _route_kernel(x_ref, or_ref, oa_ref, mod_ref, wo_ref, gain_ref, rwh_ref, rwl_ref,
                  rb_ref, tri_ref, xo_ref, hf_ref, topi_ref, gate_ref, rank_ref, cnt_ref):
    tm = x_ref.shape[0]
    mod = mod_ref[0]
    d = D_MODEL
    mix = _dot(or_ref[...], wo_ref[0:RET_WIDTH, :]) + _dot(oa_ref[...], wo_ref[RET_WIDTH:, :])
    x1 = x_ref[...] + mod[:, 2 * d:3 * d] * mix
    xo_ref[...] = x1
    hf = _modulate(x1, gain_ref[...], mod[:, 3 * d:4 * d], mod[:, 4 * d:5 * d])
    for c in range(ROW_SLABS):
        hf_ref[pl.ds(c, tm, stride=ROW_SLABS), :] = hf[:, c * LANES:(c + 1) * LANES]
    h_hi, h_lo = _split_bf16(hf)
    rwh = rwh_ref[...]
    logits = _dot(h_hi, rwh) + _dot(h_lo, rwh) + _dot(h_hi, rwl_ref[...]) + rb_ref[...]
    lane = lax.broadcasted_iota(I32, logits.shape, 1)
    work = logits
    vals, hots = [], []
    topi = jnp.zeros(logits.shape, I32)
    for k in range(TOP_K):
        m = jnp.max(work, axis=-1, keepdims=True)
        idx = jnp.min(jnp.where(work == m, lane, LANES), axis=-1, keepdims=True)
        hot = lane == idx
        work = jnp.where(hot, -jnp.inf, work)
        vals.append(m)
        hots.append(hot)
        topi = jnp.where(lane == k, idx, topi)
    exps = [jnp.exp(v - vals[0]) for v in vals]
    den = exps[0] + exps[1] + exps[2] + exps[3]
    sel = hots[0] | hots[1] | hots[2] | hots[3]
    sel_f = jnp.where(sel, 1.0, 0.0)
    before = _dot(tri_ref[...], sel_f.astype(BF16))
    gates = jnp.zeros(logits.shape, F32)
    ranks = jnp.zeros(logits.shape, F32)
    for k in range(TOP_K):
        gates = jnp.where(lane == k, exps[k] / den, gates)
        rk = jnp.sum(jnp.where(hots[k], before, 0.0), axis=-1, keepdims=True)
        ranks = jnp.where(lane == k, rk, ranks)
    topi_ref[...] = topi
    gate_ref[...] = gates
    rank_ref[...] = ranks.astype(I32)
    cnt_ref[0] = jnp.sum(sel_f, axis=0, keepdims=True).astype(I32)


def _route_call(x2, o_r, o_a, mod, w_out, gain, rw_hi, rw_lo, rb, tri, seq, tm):
    t = x2.shape[0]
    nt = t // tm
    row = lambda i: (i, 0)
    const = lambda i: (0, 0)
    outs = [
        jax.ShapeDtypeStruct((t, D_MODEL), F32),
        jax.ShapeDtypeStruct((t * ROW_SLABS, LANES), F32),
        jax.ShapeDtypeStruct((t, LANES), I32),
        jax.ShapeDtypeStruct((t, LANES), F32),
        jax.ShapeDtypeStruct((t, LANES), I32),
        jax.ShapeDtypeStruct((nt, 1, LANES), I32),
    ]
    out_specs = [
        pl.BlockSpec((tm, D_MODEL), row),
        pl.BlockSpec((tm * ROW_SLABS, LANES), row),
        pl.BlockSpec((tm, LANES), row),
        pl.BlockSpec((tm, LANES), row),
        pl.BlockSpec((tm, LANES), row),
        pl.BlockSpec((1, 1, LANES), lambda i: (i, 0, 0)),
    ]
    return pl.pallas_call(
        _route_kernel,
        grid=(nt,),
        in_specs=[
            pl.BlockSpec((tm, D_MODEL), row),
            pl.BlockSpec((tm, RET_WIDTH), row),
            pl.BlockSpec((tm, ATT_WIDTH), row),
            pl.BlockSpec((1, 1, 6 * D_MODEL), lambda i: ((i * tm) // seq, 0, 0)),
            pl.BlockSpec((D_MODEL, D_MODEL), const),
            pl.BlockSpec((1, D_MODEL), const),
            pl.BlockSpec((D_MODEL, LANES), const),
            pl.BlockSpec((D_MODEL, LANES), const),
            pl.BlockSpec((1, LANES), const),
            pl.BlockSpec((tm, tm), const),
        ],
        out_specs=out_specs,
        out_shape=outs,
        compiler_params=_cparams(("parallel",)),
        name="out_proj_route",
    )(x2, o_r, o_a, mod, w_out, gain, rw_hi, rw_lo, rb, tri)


def _row_slab(ref, row8):
    return ref.at[pl.ds(pl.multiple_of(row8, SUBLANES), SUBLANES), :]


def _dispatch_kernel(dest_ref, src_ref, xs_in_ref, xs_ref, sem):
    del xs_in_ref
    tm = src_ref.shape[0] // ROW_SLABS

    def issue(t, carry):
        for k in range(TOP_K):
            pltpu.make_async_copy(_row_slab(src_ref, t * ROW_SLABS),
                                  _row_slab(xs_ref, dest_ref[k, t]), sem).start()
        return carry

    lax.fori_loop(0, tm, issue, 0)
    for k in range(TOP_K):
        pltpu.make_async_copy(src_ref, xs_ref.at[pl.ds(0, tm * ROW_SLABS), :], sem).wait()


def _dispatch_call(dest8, hf_tok, xs_prev, tm):
    t = hf_tok.shape[0] // ROW_SLABS
    return pl.pallas_call(
        _dispatch_kernel,
        grid=(t // tm,),
        in_specs=[
            pl.BlockSpec((TOP_K, tm), lambda i: (0, i), memory_space=pltpu.SMEM),
            pl.BlockSpec((tm * ROW_SLABS, LANES), lambda i: (i, 0)),
            pl.BlockSpec(memory_space=pl.ANY),
        ],
        out_specs=pl.BlockSpec(memory_space=pl.ANY),
        out_shape=jax.ShapeDtypeStruct(xs_prev.shape, F32),
        scratch_shapes=[pltpu.SemaphoreType.DMA(())],
        input_output_aliases={2: 0},
        compiler_params=_cparams(("arbitrary",)),
        name="moe_dispatch",
    )(dest8, hf_tok, xs_prev)


def _moe_kernel(be_ref, nu_ref, xs_ref, wg_ref, wl_ref, bg_ref, bl_ref, wd_ref, bd_ref,
                y_ref, xb_ref):
    del be_ref
    i = pl.program_id(0)
    tm = xb_ref.shape[0]

    @pl.when(i < nu_ref[0])
    def _():
        for c in range(ROW_SLABS):
            xb_ref[:, c * LANES:(c + 1) * LANES] = (
                xs_ref[pl.ds(c, tm, stride=ROW_SLABS), :].astype(BF16))
        x = xb_ref[...]
        g = _dot(x, wg_ref[0]) + bg_ref[0]
        lin = _dot(x, wl_ref[0]) + bl_ref[0]
        glu = jnp.minimum(g, SWIGLU_LIMIT)
        lin = jnp.clip(lin, -SWIGLU_LIMIT, SWIGLU_LIMIT)
        act = (lin + 1.0) * glu * _sigmoid(SWIGLU_ALPHA * glu)
        y = _dot(act.astype(BF16), wd_ref[0]) + bd_ref[0]
        for c in range(ROW_SLABS):
            y_ref[pl.ds(c, tm, stride=ROW_SLABS), :] = y[:, c * LANES:(c + 1) * LANES]

    @pl.when(i >= nu_ref[0])
    def _():
        y_ref[...] = jnp.zeros_like(y_ref)


def _moe_call(block_e, n_used, xs, wg, wl, bg, bl, wd, bd, tm):
    n_tiles = xs.shape[0] // (tm * ROW_SLABS)
    row = lambda i, be, nu: (i, 0)
    wsel = lambda i, be, nu: (be[i], 0, 0)
    grid_spec = pltpu.PrefetchScalarGridSpec(
        num_scalar_prefetch=2,
        grid=(n_tiles,),
        in_specs=[
            pl.BlockSpec((tm * ROW_SLABS, LANES), row),
            pl.BlockSpec((1, D_MODEL, D_FF), wsel),
            pl.BlockSpec((1, D_MODEL, D_FF), wsel),
            pl.BlockSpec((1, 1, D_FF), wsel),
            pl.BlockSpec((1, 1, D_FF), wsel),
            pl.BlockSpec((1, D_FF, D_MODEL), wsel),
            pl.BlockSpec((1, 1, D_MODEL), wsel),
        ],
        out_specs=pl.BlockSpec((tm * ROW_SLABS, LANES), row),
        scratch_shapes=[pltpu.VMEM((tm, D_MODEL), BF16)],
    )
    return pl.pallas_call(
        _moe_kernel,
        grid_spec=grid_spec,
        out_shape=jax.ShapeDtypeStruct(xs.shape, F32),
        compiler_params=_cparams(("arbitrary",)),
        name="moe_experts",
    )(block_e, n_used, xs, wg, wl, bg, bl, wd, bd)


def _combine_kernel(dest_ref, x_ref, gate_ref, mod_ref, y_ref, o_ref, buf_ref, sem):
    tm = x_ref.shape[0]

    def issue(t, carry):
        for k in range(TOP_K):
            pltpu.make_async_copy(_row_slab(y_ref, dest_ref[k, t]),
                                  _row_slab(buf_ref.at[k], t * ROW_SLABS), sem).start()
        return carry

    lax.fori_loop(0, tm, issue, 0)
    for k in range(TOP_K):
        pltpu.make_async_copy(y_ref.at[pl.ds(0, tm * ROW_SLABS), :], buf_ref.at[k], sem).wait()
    mod = mod_ref[0]
    gates = gate_ref[...]
    for c in range(ROW_SLABS):
        acc = jnp.zeros((tm, LANES), F32)
        for k in range(TOP_K):
            acc = acc + gates[:, k:k + 1] * buf_ref[k, pl.ds(c, tm, stride=ROW_SLABS), :]
        sl = slice(c * LANES, (c + 1) * LANES)
        o_ref[:, sl] = x_ref[:, sl] + mod[:, 5 * D_MODEL + c * LANES:5 * D_MODEL + (c + 1) * LANES] * acc


def _combine_call(dest8, x2, gates, mod, y, seq, tm):
    t = x2.shape[0]
    row = lambda i: (i, 0)
    return pl.pallas_call(
        _combine_kernel,
        grid=(t // tm,),
        in_specs=[
            pl.BlockSpec((TOP_K, tm), lambda i: (0, i), memory_space=pltpu.SMEM),
            pl.BlockSpec((tm, D_MODEL), row),
            pl.BlockSpec((tm, LANES), row),
            pl.BlockSpec((1, 1, 6 * D_MODEL), lambda i: ((i * tm) // seq, 0, 0)),
            pl.BlockSpec(memory_space=pl.ANY),
        ],
        out_specs=pl.BlockSpec((tm, D_MODEL), row),
        out_shape=jax.ShapeDtypeStruct((t, D_MODEL), F32),
        scratch_shapes=[pltpu.VMEM((TOP_K, tm * ROW_SLABS, LANES), F32),
                        pltpu.SemaphoreType.DMA(())],
        compiler_params=_cparams(("arbitrary",)),
        name="moe_combine",
    )(dest8, x2, gates, mod, y)


def _plan(topi, rank, cnt, tm_route, tm_moe, n_tiles_moe):
    t = topi.shape[0]
    cnt = cnt[:, 0, :N_EXPERTS]
    totals = jnp.sum(cnt, axis=0)
    padded = (totals + tm_moe - 1) // tm_moe * tm_moe
    pad_end = jnp.cumsum(padded)
    pad_start = pad_end - padded
    tile_base = pad_start[None, :] + jnp.cumsum(cnt, axis=0) - cnt
    top = topi[:, :TOP_K]
    base_tok = jnp.repeat(tile_base, tm_route, axis=0)
    hot = top[:, :, None] == jnp.arange(N_EXPERTS, dtype=I32)[None, None, :]
    base = jnp.sum(jnp.where(hot, base_tok[:, None, :], 0), axis=-1)
    dest = base + rank[:, :TOP_K]
    dest8 = (dest * ROW_SLABS).T.astype(I32)
    tile_start = jnp.arange(n_tiles_moe, dtype=I32) * tm_moe
    block_e = jnp.minimum(jnp.sum(tile_start[:, None] >= pad_end[None, :], axis=1),
                          N_EXPERTS - 1).astype(I32)
    n_used = (pad_end[-1] // tm_moe).astype(I32).reshape(1)
    return dest8, block_e, n_used


def _pick(n, pref):
    for c in pref:
        if n % c == 0:
            return c
    raise ValueError(f"no tile size in {pref} divides {n}")


def _trunk(x, mods, prm):
    batch, seq, _ = x.shape
    t = batch * seq
    assert seq % RET_CHUNK == 0 and seq % ATT_BLOCK == 0
    tm_in = _pick(seq, (512, 256, 128))
    tm_route = _pick(seq, (512, 256, 128))
    tm_disp = _pick(t, (512, 256, 128))
    tm_comb = _pick(seq, (256, 128))
    tm_moe = 512 if t * TOP_K >= 65536 * 4 else (256 if t * TOP_K >= 16384 else 128)
    n_tiles_moe = (t * TOP_K) // tm_moe + N_EXPERTS
    x2 = x.reshape(t, D_MODEL)
    xs = jnp.zeros((n_tiles_moe * tm_moe * ROW_SLABS, LANES), F32)
    tri = (jnp.arange(tm_route)[:, None] > jnp.arange(tm_route)[None, :]).astype(BF16)
    depth = prm["w_in"].shape[0]
    for l in range(depth):
        mod = mods[l]
        qr, kr, vr, sg, qa, ka, va = _in_call(
            x2, mod, prm["norm_mix_g"][l], prm["w_in"][l], prm["g128"],
            prm["q_g"][l], prm["k_g"][l], seq, tm_in)
        o_r = _ret_call(qr, kr, vr, sg, prm["lgf"][l], prm["lgb"][l], prm["g128"], batch, seq)
        o_a = _attn_call(qa, ka, va, prm["sink"][l], prm["bias"], batch, seq)
        x2, hf_tok, topi, gates, rank, cnt = _route_call(
            x2, o_r, o_a, mod, prm["w_out"][l], prm["norm_ffn_g"][l],
            prm["rw_hi"][l], prm["rw_lo"][l], prm["rb"][l], tri, seq, tm_route)
        dest8, block_e, n_used = _plan(topi, rank, cnt, tm_route, tm_moe, n_tiles_moe)
        xs = _dispatch_call(dest8, hf_tok, xs, tm_disp)
        y = _moe_call(block_e, n_used, xs, prm["w_g"][l], prm["w_l"][l], prm["b_g"][l],
                      prm["b_l"][l], prm["w_dn"][l], prm["b_dn"][l], tm_moe)
        x2 = _combine_call(dest8, x2, gates, mod, y, seq, tm_comb)
    return x2.reshape(batch, seq, D_MODEL)


def _prepare(ada_w, ada_b, norm_mix_g, w_in, ret_log_gamma_f, ret_log_gamma_b, q_norm_g,
             k_norm_g, attn_sink, w_out, norm_ffn_g, router_w, router_b, w_gu, b_gu, w_dn, b_dn):
    depth = w_in.shape[0]
    pad_e = LANES - N_EXPERTS
    rw = jnp.pad(router_w, ((0, 0), (0, 0), (0, pad_e)))
    rw_hi = rw.astype(BF16)
    rw_lo = (rw - rw_hi.astype(F32)).astype(BF16)
    rb = jnp.pad(router_b, ((0, 0), (0, pad_e)), constant_values=NEG_INF).reshape(depth, 1, LANES)
    lane = jnp.arange(LANES)
    g128 = jnp.where((lane[:, None] // HEAD_DIM) == (lane[None, :] // HEAD_DIM),
                     1.0 / HEAD_DIM, 0.0).astype(BF16)
    w_g, w_l = _deint_call(w_gu)
    return dict(
        norm_mix_g=norm_mix_g.reshape(depth, 1, D_MODEL),
        norm_ffn_g=norm_ffn_g.reshape(depth, 1, D_MODEL),
        w_in=w_in.astype(BF16),
        w_out=w_out.astype(BF16),
        lgf=ret_log_gamma_f.astype(F32),
        lgb=ret_log_gamma_b.astype(F32),
        q_g=jnp.tile(q_norm_g, (1, LANES // HEAD_DIM)).reshape(depth, 1, LANES),
        k_g=jnp.tile(k_norm_g, (1, LANES // HEAD_DIM)).reshape(depth, 1, LANES),
        sink=attn_sink.astype(F32),
        rw_hi=rw_hi, rw_lo=rw_lo, rb=rb,
        w_g=w_g,
        w_l=w_l,
        b_g=b_gu[..., 0::2].reshape(depth, N_EXPERTS, 1, D_FF),
        b_l=b_gu[..., 1::2].reshape(depth, N_EXPERTS, 1, D_FF),
        w_dn=w_dn.astype(BF16),
        b_dn=b_dn.reshape(depth, N_EXPERTS, 1, D_MODEL),
        g128=g128,
        bias=_attn_bias(),
    )


def kernel(x_prompt, x_sample, c_prompt, c_sample, ada_w, ada_b, norm_mix_g, w_in, ret_log_gamma_f, ret_log_gamma_b, q_norm_g, k_norm_g, attn_sink, w_out, norm_ffn_g, router_w, router_b, w_gu, b_gu, w_dn, b_dn):
    prm = _prepare(ada_w, ada_b, norm_mix_g, w_in, ret_log_gamma_f, ret_log_gamma_b, q_norm_g,
                   k_norm_g, attn_sink, w_out, norm_ffn_g, router_w, router_b, w_gu, b_gu,
                   w_dn, b_dn)
    bp, bs = c_prompt.shape[0], c_sample.shape[0]
    n_c = bp + bs
    n_pad = -n_c % SUBLANES
    c_all = jnp.concatenate([c_prompt, c_sample, jnp.zeros((n_pad, D_MODEL), F32)], axis=0)
    mods = _ada_call(c_all, ada_w, ada_b)
    depth = ada_w.shape[0]
    mods_p = mods[:, :bp].reshape(depth, bp, 1, 6 * D_MODEL)
    mods_s = mods[:, bp:n_c].reshape(depth, bs, 1, 6 * D_MODEL)
    y_prompt = _trunk(x_prompt, mods_p, prm)
    y_sample = _trunk(x_sample, mods_s, prm)
    return (y_prompt, y_sample)
```

```python
import functools

import jax
import jax.numpy as jnp
from jax import lax
from jax.experimental import pallas as pl
from jax.experimental.pallas import tpu as pltpu

F32 = jnp.float32
BF16 = jnp.bfloat16
I32 = jnp.int32

D_MODEL = 1024
RET_HEADS = 8
HEAD_DIM = 64
RET_WIDTH = RET_HEADS * HEAD_DIM
RET_CHUNK = 128
ATT_HEADS = 8
ATT_KV_HEADS = 2
ATT_WIDTH = ATT_HEADS * HEAD_DIM
ATT_KV_WIDTH = ATT_KV_HEADS * HEAD_DIM
WINDOW = 128
ATT_BLOCK = 128
IN_WIDTH = 4 * RET_WIDTH + ATT_WIDTH + 2 * ATT_KV_WIDTH
N_EXPERTS = 32
TOP_K = 4
D_FF = D_MODEL
SWIGLU_LIMIT = 7.0
SWIGLU_ALPHA = 1.702
RMS_EPS = 1e-6
NEG_INF = -1e30

LANES = 128
SUBLANES = 8
ROW_SLABS = D_MODEL // LANES
VMEM_LIMIT = 56 * 1024 * 1024


def _cparams(sem):
    return pltpu.CompilerParams(dimension_semantics=sem, vmem_limit_bytes=VMEM_LIMIT)


def _sigmoid(x):
    return 1.0 / (1.0 + jnp.exp(-x))


def _split_bf16(x):
    hi = x.astype(BF16)
    lo = (x - hi.astype(F32)).astype(BF16)
    return hi, lo


def _dot(a, b):
    return jnp.dot(a, b, preferred_element_type=F32)


def _dot_nt(a, b):
    return lax.dot_general(a, b, (((1,), (1,)), ((), ())), preferred_element_type=F32)


def _group_mean(x2, g_ref):
    hi, lo = _split_bf16(x2)
    g = g_ref[...]
    return _dot(hi, g) + _dot(lo, g)


def _ada_kernel(c_ref, w_ref, b_ref, o_ref):
    c = c_ref[...]
    a = c * _sigmoid(c)
    a_hi, a_lo = _split_bf16(a)
    w_hi, w_lo = _split_bf16(w_ref[0])
    acc = _dot(a_hi, w_hi) + _dot(a_lo, w_hi) + _dot(a_hi, w_lo)
    o_ref[0] = acc + b_ref[0]


def _ada_call(c, ada_w, ada_b):
    depth = ada_w.shape[0]
    bp = c.shape[0]
    tn = 1536
    n_out = ada_w.shape[2]
    return pl.pallas_call(
        _ada_kernel,
        grid=(depth, n_out // tn),
        in_specs=[
            pl.BlockSpec((bp, D_MODEL), lambda l, j: (0, 0)),
            pl.BlockSpec((1, D_MODEL, tn), lambda l, j: (l, 0, j)),
            pl.BlockSpec((1, 1, tn), lambda l, j: (l, 0, j)),
        ],
        out_specs=pl.BlockSpec((1, bp, tn), lambda l, j: (l, 0, j)),
        out_shape=jax.ShapeDtypeStruct((depth, bp, n_out), F32),
        compiler_params=_cparams(("arbitrary", "arbitrary")),
        name="ada_mod",
    )(c, ada_w, ada_b.reshape(depth, 1, n_out))


DEINT_COLS = 256


def _deint_kernel(w_ref, p_ref, g_ref, l_ref):
    perm = p_ref[...]
    half = DEINT_COLS // 2
    for j in range(w_ref.shape[2] // DEINT_COLS):
        w = w_ref[0, :, j * DEINT_COLS:(j + 1) * DEINT_COLS].astype(BF16)
        r = _dot(w, perm)
        g_ref[0, :, j * half:(j + 1) * half] = r[:, :half].astype(BF16)
        l_ref[0, :, j * half:(j + 1) * half] = r[:, half:].astype(BF16)


def _deint_call(w_gu):
    depth, n_e, d_in, d_out2 = w_gu.shape
    w = w_gu.reshape(depth * n_e, d_in, d_out2)
    tk = 512
    src = jnp.arange(DEINT_COLS)
    dst = jnp.where(src % 2 == 0, src // 2, DEINT_COLS // 2 + src // 2)
    perm = (dst[:, None] == jnp.arange(DEINT_COLS)[None, :]).astype(BF16)
    out = jax.ShapeDtypeStruct((depth * n_e, d_in, d_out2 // 2), BF16)
    blk = lambda e, i: (e, i, 0)
    w_g, w_l = pl.pallas_call(
        _deint_kernel,
        grid=(depth * n_e, d_in // tk),
        in_specs=[pl.BlockSpec((1, tk, d_out2), blk),
                  pl.BlockSpec((DEINT_COLS, DEINT_COLS), lambda e, i: (0, 0))],
        out_specs=[pl.BlockSpec((1, tk, d_out2 // 2), blk)] * 2,
        out_shape=[out, out],
        compiler_params=_cparams(("parallel", "parallel")),
        name="expert_weight_split",
    )(w, perm)
    shape = (depth, n_e, d_in, d_out2 // 2)
    return w_g.reshape(shape), w_l.reshape(shape)


def _modulate(x, gain, shift, scale):
    ms = jnp.mean(x * x, axis=-1, keepdims=True)
    xn = x * lax.rsqrt(ms + RMS_EPS)
    return xn * gain * (1.0 + scale) + shift


def _in_kernel(x_ref, mod_ref, gain_ref, w_ref, g_ref, qg_ref, kg_ref,
               qr_ref, kr_ref, vr_ref, sg_ref, qa_ref, ka_ref, va_ref):
    mod = mod_ref[0]
    h = _modulate(x_ref[...], gain_ref[...], mod[:, 0:D_MODEL], mod[:, D_MODEL:2 * D_MODEL])
    proj = _dot(h.astype(BF16), w_ref[...])
    w = RET_WIDTH
    qr_ref[...] = proj[:, 0:w].astype(BF16)
    kr_ref[...] = (proj[:, w:2 * w] * (HEAD_DIM ** -0.5)).astype(BF16)
    vr_ref[...] = proj[:, 2 * w:3 * w].astype(BF16)
    gr = proj[:, 3 * w:4 * w]
    sg_ref[...] = (gr * _sigmoid(gr)).astype(BF16)
    base = 4 * w
    qg = qg_ref[...] * (HEAD_DIM ** -0.5)
    for s in range(ATT_WIDTH // LANES):
        q = proj[:, base + s * LANES: base + (s + 1) * LANES]
        msq = _group_mean(q * q, g_ref)
        qa_ref[:, s * LANES:(s + 1) * LANES] = (q * lax.rsqrt(msq + RMS_EPS) * qg).astype(BF16)
    kb = base + ATT_WIDTH
    k = proj[:, kb:kb + LANES]
    msk = _group_mean(k * k, g_ref)
    kn = k * lax.rsqrt(msk + RMS_EPS) * kg_ref[...]
    v = proj[:, kb + LANES:kb + 2 * LANES]
    lane = lax.broadcasted_iota(I32, kn.shape, 1)
    first = lane < HEAD_DIM
    for src, dst in ((kn, ka_ref), (v, va_ref)):
        rot = pltpu.roll(src, HEAD_DIM, 1)
        dst[:, 0:LANES] = jnp.where(first, src, rot).astype(BF16)
        dst[:, LANES:2 * LANES] = jnp.where(first, rot, src).astype(BF16)


def _in_call(x2, mod, gain, w_in, g128, qg, kg, seq, tm):
    t = x2.shape[0]
    row = lambda i: (i, 0)
    const = lambda i: (0, 0)
    outs = [
        jax.ShapeDtypeStruct((t, RET_WIDTH), BF16),
        jax.ShapeDtypeStruct((t, RET_WIDTH), BF16),
        jax.ShapeDtypeStruct((t, RET_WIDTH), BF16),
        jax.ShapeDtypeStruct((t, RET_WIDTH), BF16),
        jax.ShapeDtypeStruct((t, ATT_WIDTH), BF16),
        jax.ShapeDtypeStruct((t, 2 * LANES), BF16),
        jax.ShapeDtypeStruct((t, 2 * LANES), BF16),
    ]
    return pl.pallas_call(
        _in_kernel,
        grid=(t // tm,),
        in_specs=[
            pl.BlockSpec((tm, D_MODEL), row),
            pl.BlockSpec((1, 1, 6 * D_MODEL), lambda i: ((i * tm) // seq, 0, 0)),
            pl.BlockSpec((1, D_MODEL), const),
            pl.BlockSpec((D_MODEL, IN_WIDTH), const),
            pl.BlockSpec((LANES, LANES), const),
            pl.BlockSpec((1, LANES), const),
            pl.BlockSpec((1, LANES), const),
        ],
        out_specs=[pl.BlockSpec((tm, o.shape[1]), row) for o in outs],
        out_shape=outs,
        compiler_params=_cparams(("parallel",)),
        name="in_proj",
    )(x2, mod, gain, w_in, g128, qg, kg)


def _ret_kernel(lgf_ref, lgb_ref, q_ref, k_ref, v_ref, sg_ref, g_ref, o_ref,
                uf_ref, ub_ref, sf_ref, sb_ref, s_ref, acc_ref):
    p = pl.program_id(1)
    c = RET_CHUNK
    n_chunks = q_ref.shape[0] // c
    lf0, lf1 = lgf_ref[2 * p], lgf_ref[2 * p + 1]
    lb0, lb1 = lgb_ref[2 * p], lgb_ref[2 * p + 1]
    row = lax.broadcasted_iota(I32, (c, LANES), 0)
    lane = lax.broadcasted_iota(I32, (c, LANES), 1)
    pos = row.astype(F32)
    first = lane < HEAD_DIM
    lf_lane = jnp.where(first, lf0, lf1)
    lb_lane = jnp.where(first, lb0, lb1)
    qd_f = jnp.exp(lf_lane * (pos + 1.0))
    kd_f = jnp.exp(lf_lane * (c - 1.0 - pos))
    qd_b = jnp.exp(lb_lane * (c - pos))
    kd_b = jnp.exp(lb_lane * pos)
    cd_f = jnp.exp(lf_lane * float(c))
    cd_b = jnp.exp(lb_lane * float(c))
    same_head = (row < HEAD_DIM) == first
    row2 = lax.broadcasted_iota(I32, (c, 2 * c), 0)
    col2 = lax.broadcasted_iota(I32, (c, 2 * c), 1)
    head1 = col2 >= c
    diff = (row2 - jnp.where(head1, col2 - c, col2)).astype(F32)
    lf_c = jnp.where(head1, lf1, lf0)
    lb_c = jnp.where(head1, lb1, lb0)
    decay = jnp.where(diff >= 0, jnp.exp(lf_c * jnp.maximum(diff, 0.0)),
                      jnp.exp(lb_c * jnp.maximum(-diff, 0.0)))

    unroll = _pick(n_chunks, (4, 2, 1))

    def updates(g, carry):
        for u in range(unroll):
            n = g * unroll + u
            off = pl.multiple_of(n * c, c)
            k = k_ref[pl.ds(off, c), :].astype(F32)
            vt = v_ref[pl.ds(off, c), :].astype(F32).T.astype(BF16)
            uf_ref[n] = jnp.where(same_head, _dot(vt, (k * kd_f).astype(BF16)), 0.0)
            ub_ref[n] = jnp.where(same_head, _dot(vt, (k * kd_b).astype(BF16)), 0.0)
        return carry

    lax.fori_loop(0, n_chunks // unroll, updates, 0)

    def scan_f(n, state):
        sf_ref[n] = state.astype(BF16)
        return state * cd_f + uf_ref[n]

    def scan_b(i, state):
        n = n_chunks - 1 - i
        sb_ref[n] = state.astype(BF16)
        return state * cd_b + ub_ref[n]

    zero_state = jnp.zeros((LANES, LANES), F32)
    lax.fori_loop(0, n_chunks, scan_f, zero_state)
    lax.fori_loop(0, n_chunks, scan_b, zero_state)

    group = s_ref.shape[0]
    zero = jnp.zeros((c, LANES), BF16)

    def split_heads(x):
        return jnp.concatenate([jnp.where(first, x, zero), jnp.where(first, zero, x)], axis=0)

    def outputs(g, carry):
        offs = [pl.multiple_of((g * group + u) * c, c) for u in range(group)]
        for u in range(group):
            s_ref[u] = _dot_nt(q_ref[pl.ds(offs[u], c), :],
                               split_heads(k_ref[pl.ds(offs[u], c), :]))
        for u in range(group):
            n = g * group + u
            q = q_ref[pl.ds(offs[u], c), :].astype(F32)
            acc_ref[u] = (_dot((s_ref[u] * decay).astype(BF16), split_heads(v_ref[pl.ds(offs[u], c), :]))
                          + _dot_nt((q * qd_f).astype(BF16), sf_ref[n])
                          + _dot_nt((q * qd_b).astype(BF16), sb_ref[n]))
        for u in range(group):
            o = acc_ref[u]
            ms = _group_mean(o * o, g_ref)
            on = o * lax.rsqrt(ms + RMS_EPS)
            o_ref[pl.ds(offs[u], c), :] = (on * sg_ref[pl.ds(offs[u], c), :].astype(F32)).astype(BF16)
        return carry

    lax.fori_loop(0, n_chunks // group, outputs, 0)


def _ret_call(qr, kr, vr, sg, lgf, lgb, g128, batch, seq):
    t = qr.shape[0]
    blk = lambda b, p: (b, p)
    smem = pl.BlockSpec(memory_space=pltpu.SMEM)
    n_chunks = seq // RET_CHUNK
    group = _pick(n_chunks, (8, 4, 2, 1))
    return pl.pallas_call(
        _ret_kernel,
        grid=(batch, RET_HEADS // 2),
        in_specs=[smem, smem] + [pl.BlockSpec((seq, LANES), blk)] * 4
                 + [pl.BlockSpec((LANES, LANES), lambda b, p: (0, 0))],
        out_specs=pl.BlockSpec((seq, LANES), blk),
        out_shape=jax.ShapeDtypeStruct((t, RET_WIDTH), BF16),
        scratch_shapes=[pltpu.VMEM((n_chunks, LANES, LANES), F32),
                        pltpu.VMEM((n_chunks, LANES, LANES), F32),
                        pltpu.VMEM((n_chunks, LANES, LANES), BF16),
                        pltpu.VMEM((n_chunks, LANES, LANES), BF16),
                        pltpu.VMEM((group, RET_CHUNK, 2 * RET_CHUNK), F32),
                        pltpu.VMEM((group, RET_CHUNK, LANES), F32)],
        compiler_params=_cparams(("parallel", "parallel")),
        name="retention",
    )(lgf, lgb, qr, kr, vr, sg, g128)


def _attn_kernel(sink_ref, q_ref, kp_ref, kc_ref, kn_ref, vp_ref, vc_ref, vn_ref,
                 bias_ref, o_ref):
    j = pl.program_id(1)
    nb = pl.num_programs(1)
    blk = ATT_BLOCK
    col = lax.broadcasted_iota(I32, (blk, 3 * blk), 1)
    outside = ((col < blk) & (j == 0)) | ((col >= 2 * blk) & (j == nb - 1))
    edge = jnp.where(outside, NEG_INF, 0.0)
    lane = lax.broadcasted_iota(I32, (3 * blk, LANES), 1)
    first = lane < HEAD_DIM
    out_first = lax.broadcasted_iota(I32, (blk, LANES), 1) < HEAD_DIM
    kk = jnp.concatenate([kp_ref[...], kc_ref[...], kn_ref[...]], axis=0)
    vv = jnp.concatenate([vp_ref[...], vc_ref[...], vn_ref[...]], axis=0)
    zero = jnp.zeros((3 * blk, LANES), BF16)
    pairs_per_kv = ATT_HEADS // ATT_KV_HEADS // 2
    for kv in range(ATT_KV_HEADS):
        kd = kk[:, kv * LANES:(kv + 1) * LANES]
        vd = vv[:, kv * LANES:(kv + 1) * LANES]
        kcat = jnp.concatenate([jnp.where(first, kd, zero), jnp.where(first, zero, kd)], axis=0)
        vcat = jnp.concatenate([jnp.where(first, vd, zero), jnp.where(first, zero, vd)], axis=0)
        for pp in range(pairs_per_kv):
            hp = kv * pairs_per_kv + pp
            qp = q_ref[:, hp * LANES:(hp + 1) * LANES]
            s = _dot_nt(qp, kcat)
            probs, inv = [], []
            for hh in range(2):
                h = 2 * hp + hh
                sh = s[:, hh * 3 * blk:(hh + 1) * 3 * blk] + bias_ref[h] + edge
                sink = sink_ref[h]
                m = jnp.maximum(jnp.max(sh, axis=-1, keepdims=True), sink)
                e = jnp.exp(sh - m)
                den = jnp.sum(e, axis=-1, keepdims=True) + jnp.exp(sink - m)
                probs.append(e.astype(BF16))
                inv.append(1.0 / den)
            o = _dot(jnp.concatenate(probs, axis=1), vcat)
            o = o * jnp.where(out_first, inv[0], inv[1])
            o_ref[:, hp * LANES:(hp + 1) * LANES] = o.astype(BF16)


def _attn_bias():
    blk = ATT_BLOCK
    i = jnp.arange(blk, dtype=F32)[:, None]
    jj = jnp.arange(3 * blk, dtype=F32)[None, :]
    rel = jnp.abs(i + blk - jj)
    slopes = 2.0 ** (-8.0 * (jnp.arange(ATT_HEADS, dtype=F32) + 1.0) / ATT_HEADS)
    return jnp.where(rel <= WINDOW, -slopes[:, None, None] * rel, NEG_INF)


def _attn_call(qa, ka, va, sink, bias, batch, seq):
    t = qa.shape[0]
    blk = ATT_BLOCK
    nb = seq // blk
    cur = lambda b, j: (b * nb + j, 0)
    prev = lambda b, j: (b * nb + jnp.maximum(j - 1, 0), 0)
    nxt = lambda b, j: (b * nb + jnp.minimum(j + 1, nb - 1), 0)
    kv_specs = [pl.BlockSpec((blk, 2 * LANES), f) for f in (prev, cur, nxt)]
    return pl.pallas_call(
        _attn_kernel,
        grid=(batch, nb),
        in_specs=[pl.BlockSpec(memory_space=pltpu.SMEM),
                  pl.BlockSpec((blk, ATT_WIDTH), cur)] + kv_specs + kv_specs
                 + [pl.BlockSpec((ATT_HEADS, blk, 3 * blk), lambda b, j: (0, 0, 0))],
        out_specs=pl.BlockSpec((blk, ATT_WIDTH), cur),
        out_shape=jax.ShapeDtypeStruct((t, ATT_WIDTH), BF16),
        compiler_params=_cparams(("parallel", "arbitrary")),
        name="window_attn",
    )(sink, qa, ka, ka, ka, va, va, va, bias)


def _route_kernel(x_ref, or_ref, oa_ref, mod_ref, wo_ref, gain_ref, rwh_ref, rwl_ref,
                  rb_ref, tri_ref, upper_ref, xo_ref, hf_ref, pos_ref, post_ref, gate_ref, cnt_ref):
    mod = mod_ref[0]
    d = D_MODEL
    mix = _dot(or_ref[...], wo_ref[0:RET_WIDTH, :]) + _dot(oa_ref[...], wo_ref[RET_WIDTH:, :])
    x1 = x_ref[...] + mod[:, 2 * d:3 * d] * mix
    xo_ref[...] = x1
    hf = _modulate(x1, gain_ref[...], mod[:, 3 * d:4 * d], mod[:, 4 * d:5 * d])
    h_hi, h_lo = _split_bf16(hf)
    hf_ref[...] = h_hi
    rwh = rwh_ref[...]
    logits = _dot(h_hi, rwh) + _dot(h_lo, rwh) + _dot(h_hi, rwl_ref[...]) + rb_ref[...]
    lane = lax.broadcasted_iota(I32, logits.shape, 1)
    work = logits
    vals, hots = [], []
    for k in range(TOP_K):
        m = jnp.max(work, axis=-1, keepdims=True)
        idx = jnp.min(jnp.where(work == m, lane, LANES), axis=-1, keepdims=True)
        hot = lane == idx
        work = jnp.where(hot, -jnp.inf, work)
        vals.append(m)
        hots.append(hot)
    exps = [jnp.exp(v - vals[0]) for v in vals]
    den = exps[0] + exps[1] + exps[2] + exps[3]
    sel_f = sum(jnp.where(h, 1.0, 0.0) for h in hots)
    before = _dot(tri_ref[...], sel_f.astype(BF16))
    counts = jnp.sum(sel_f, axis=0, keepdims=True)
    seg_off = _dot(jnp.broadcast_to(counts, (SUBLANES, LANES)).astype(BF16), upper_ref[...])[0:1]
    slot = before + seg_off
    gates = jnp.zeros(logits.shape, F32)
    pos = jnp.zeros(logits.shape, F32)
    for k in range(TOP_K):
        gates = jnp.where(lane == k, exps[k] / den, gates)
        pk = jnp.sum(jnp.where(hots[k], slot, 0.0), axis=-1, keepdims=True)
        pos = jnp.where(lane == k, pk, pos)
    gate_ref[...] = gates
    pos_ref[...] = pos.astype(I32)
    post_ref[0] = pos.T[0:SUBLANES, :].astype(I32)
    cnt_ref[0] = counts.astype(I32)


def _route_call(x2, o_r, o_a, mod, w_out, gain, rw_hi, rw_lo, rb, tri, upper, seq, tm):
    t = x2.shape[0]
    nt = t // tm
    row = lambda i: (i, 0)
    const = lambda i: (0, 0)
    outs = [
        jax.ShapeDtypeStruct((t, D_MODEL), F32),
        jax.ShapeDtypeStruct((t, D_MODEL), BF16),
        jax.ShapeDtypeStruct((t, LANES), I32),
        jax.ShapeDtypeStruct((nt, SUBLANES, tm), I32),
        jax.ShapeDtypeStruct((t, LANES), F32),
        jax.ShapeDtypeStruct((nt, 1, LANES), I32),
    ]
    out_specs = [
        pl.BlockSpec((tm, D_MODEL), row),
        pl.BlockSpec((tm, D_MODEL), row),
        pl.BlockSpec((tm, LANES), row),
        pl.BlockSpec((1, SUBLANES, tm), lambda i: (i, 0, 0)),
        pl.BlockSpec((tm, LANES), row),
        pl.BlockSpec((1, 1, LANES), lambda i: (i, 0, 0)),
    ]
    return pl.pallas_call(
        _route_kernel,
        grid=(nt,),
        in_specs=[
            pl.BlockSpec((tm, D_MODEL), row),
            pl.BlockSpec((tm, RET_WIDTH), row),
            pl.BlockSpec((tm, ATT_WIDTH), row),
            pl.BlockSpec((1, 1, 6 * D_MODEL), lambda i: ((i * tm) // seq, 0, 0)),
            pl.BlockSpec((D_MODEL, D_MODEL), const),
            pl.BlockSpec((1, D_MODEL), const),
            pl.BlockSpec((D_MODEL, LANES), const),
            pl.BlockSpec((D_MODEL, LANES), const),
            pl.BlockSpec((1, LANES), const),
            pl.BlockSpec((tm, tm), const),
            pl.BlockSpec((LANES, LANES), const),
        ],
        out_specs=out_specs,
        out_shape=outs,
        compiler_params=_cparams(("parallel",)),
        name="out_proj_route",
    )(x2, o_r, o_a, mod, w_out, gain, rw_hi, rw_lo, rb, tri, upper)


def _rows(ref, row, n):
    return ref.at[pl.ds(pl.multiple_of(row * ROW_SLABS, SUBLANES), n * ROW_SLABS), :]


def _segment_copies(cnt_ref, off_ref, dst_ref, tm, make_copy):
    def per_expert(e, carry):
        n = cnt_ref[0, 0, e]
        off = off_ref[0, 0, e]
        dst = dst_ref[0, 0, e]
        done = jnp.int32(0)
        piece = tm
        while piece >= 1:
            @pl.when((n & piece) != 0)
            def _(piece=piece, done=done):
                make_copy(off + done, dst + done, piece).start()
            done = done + (n & piece)
            piece //= 2
        return carry

    lax.fori_loop(0, N_EXPERTS, per_expert, 0)


def _dispatch_kernel(cnt_ref, off_ref, dst_ref, post_ref, hf_ref, xs_in_ref, xs_ref, stage_ref, sems):
    del xs_in_ref
    i = pl.program_id(0)
    tm = hf_ref.shape[0]
    n_slots = TOP_K * tm
    slot = i % 2
    stage = stage_ref.at[slot]
    post = post_ref[0]
    j = lax.broadcasted_iota(I32, (n_slots, tm), 0)
    perm = sum(jnp.where(post[k:k + 1, :] == j, 1.0, 0.0) for k in range(TOP_K))
    srt = _dot(perm.astype(BF16), hf_ref[...])
    for c in range(ROW_SLABS):
        stage[pl.ds(c, n_slots, stride=ROW_SLABS), :] = srt[:, c * LANES:(c + 1) * LANES]
    _segment_copies(cnt_ref, off_ref, dst_ref, tm,
                    lambda o, d, n: pltpu.make_async_copy(_rows(stage, o, n), _rows(xs_ref, d, n),
                                                          sems.at[slot]))

    def wait_tile(s):
        pltpu.make_async_copy(stage_ref.at[s], _rows(xs_ref, 0, n_slots), sems.at[s]).wait()

    @pl.when(i > 0)
    def _():
        wait_tile(1 - slot)

    @pl.when(i == pl.num_programs(0) - 1)
    def _():
        wait_tile(slot)


def _seg_spec(index_map):
    return pl.BlockSpec((1, 1, LANES), index_map, memory_space=pltpu.SMEM)


def _dispatch_call(cnt, seg_off, seg_dst, post, hf, xs_prev, tm):
    t = hf.shape[0]
    tile = lambda i: (i, 0, 0)
    return pl.pallas_call(
        _dispatch_kernel,
        grid=(t // tm,),
        in_specs=[
            _seg_spec(tile), _seg_spec(tile), _seg_spec(tile),
            pl.BlockSpec((1, SUBLANES, tm), tile),
            pl.BlockSpec((tm, D_MODEL), lambda i: (i, 0)),
            pl.BlockSpec(memory_space=pl.ANY),
        ],
        out_specs=pl.BlockSpec(memory_space=pl.ANY),
        out_shape=jax.ShapeDtypeStruct(xs_prev.shape, F32),
        scratch_shapes=[pltpu.VMEM((2, TOP_K * tm * ROW_SLABS, LANES), F32),
                        pltpu.SemaphoreType.DMA((2,))],
        input_output_aliases={5: 0},
        compiler_params=_cparams(("arbitrary",)),
        name="moe_dispatch",
    )(cnt, seg_off, seg_dst, post, hf, xs_prev)


def _moe_kernel(be_ref, nu_ref, xs_ref, wg_ref, wl_ref, bg_ref, bl_ref, wd_ref, bd_ref,
                y_ref, xb_ref):
    del be_ref
    i = pl.program_id(0)
    tm = xb_ref.shape[0]

    @pl.when(i < nu_ref[0])
    def _():
        for c in range(ROW_SLABS):
            xb_ref[:, c * LANES:(c + 1) * LANES] = (
                xs_ref[pl.ds(c, tm, stride=ROW_SLABS), :].astype(BF16))
        x = xb_ref[...]
        g = _dot(x, wg_ref[0]) + bg_ref[0]
        lin = _dot(x, wl_ref[0]) + bl_ref[0]
        glu = jnp.minimum(g, SWIGLU_LIMIT)
        lin = jnp.clip(lin, -SWIGLU_LIMIT, SWIGLU_LIMIT)
        act = (lin + 1.0) * glu * _sigmoid(SWIGLU_ALPHA * glu)
        y = _dot(act.astype(BF16), wd_ref[0]) + bd_ref[0]
        for c in range(ROW_SLABS):
            y_ref[pl.ds(c, tm, stride=ROW_SLABS), :] = y[:, c * LANES:(c + 1) * LANES]

    @pl.when(i >= nu_ref[0])
    def _():
        y_ref[...] = jnp.zeros_like(y_ref)


def _moe_call(block_e, n_used, xs, wg, wl, bg, bl, wd, bd, tm):
    n_tiles = xs.shape[0] // (tm * ROW_SLABS)
    row = lambda i, be, nu: (i, 0)
    wsel = lambda i, be, nu: (be[i], 0, 0)
    grid_spec = pltpu.PrefetchScalarGridSpec(
        num_scalar_prefetch=2,
        grid=(n_tiles,),
        in_specs=[
            pl.BlockSpec((tm * ROW_SLABS, LANES), row),
            pl.BlockSpec((1, D_MODEL, D_FF), wsel),
            pl.BlockSpec((1, D_MODEL, D_FF), wsel),
            pl.BlockSpec((1, 1, D_FF), wsel),
            pl.BlockSpec((1, 1, D_FF), wsel),
            pl.BlockSpec((1, D_FF, D_MODEL), wsel),
            pl.BlockSpec((1, 1, D_MODEL), wsel),
        ],
        out_specs=pl.BlockSpec((tm * ROW_SLABS, LANES), row),
        scratch_shapes=[pltpu.VMEM((tm, D_MODEL), BF16)],
    )
    return pl.pallas_call(
        _moe_kernel,
        grid_spec=grid_spec,
        out_shape=jax.ShapeDtypeStruct(xs.shape, F32),
        compiler_params=_cparams(("arbitrary",)),
        name="moe_experts",
    )(block_e, n_used, xs, wg, wl, bg, bl, wd, bd)


def _combine_kernel(cnt_ref, off_ref, dst_ref, cnt_n_ref, off_n_ref, dst_n_ref,
                    x_ref, pos_ref, gate_ref, mod_ref, y_ref, o_ref, buf_ref, yb_ref, sems):
    i = pl.program_id(0)
    nt = pl.num_programs(0)
    tm = x_ref.shape[0]
    n_slots = TOP_K * tm
    slot = i % 2

    def fetch(tables, s):
        buf = buf_ref.at[s]
        _segment_copies(*tables, tm,
                        lambda o, d, n: pltpu.make_async_copy(_rows(y_ref, d, n), _rows(buf, o, n),
                                                              sems.at[s]))

    @pl.when(i == 0)
    def _():
        fetch((cnt_ref, off_ref, dst_ref), slot)

    @pl.when(i + 1 < nt)
    def _():
        fetch((cnt_n_ref, off_n_ref, dst_n_ref), 1 - slot)

    pltpu.make_async_copy(_rows(y_ref, 0, n_slots), buf_ref.at[slot], sems.at[slot]).wait()
    buf = buf_ref.at[slot]
    for c in range(ROW_SLABS):
        yb_ref[:, c * LANES:(c + 1) * LANES] = buf[pl.ds(c, n_slots, stride=ROW_SLABS), :].astype(BF16)
    pos = pos_ref[...]
    gates = gate_ref[...]
    j = lax.broadcasted_iota(I32, (tm, n_slots), 1)
    wsel = sum(jnp.where(pos[:, k:k + 1] == j, gates[:, k:k + 1], 0.0) for k in range(TOP_K))
    ffn = _dot(wsel.astype(BF16), yb_ref[...])
    o_ref[...] = x_ref[...] + mod_ref[0][:, 5 * D_MODEL:6 * D_MODEL] * ffn


def _combine_call(cnt, seg_off, seg_dst, x2, pos, gates, mod, y, seq, tm):
    t = x2.shape[0]
    nt = t // tm
    row = lambda i: (i, 0)
    tile = lambda i: (i, 0, 0)
    nxt = lambda i: (jnp.minimum(i + 1, nt - 1), 0, 0)
    return pl.pallas_call(
        _combine_kernel,
        grid=(nt,),
        in_specs=[
            _seg_spec(tile), _seg_spec(tile), _seg_spec(tile),
            _seg_spec(nxt), _seg_spec(nxt), _seg_spec(nxt),
            pl.BlockSpec((tm, D_MODEL), row),
            pl.BlockSpec((tm, LANES), row),
            pl.BlockSpec((tm, LANES), row),
            pl.BlockSpec((1, 1, 6 * D_MODEL), lambda i: ((i * tm) // seq, 0, 0)),
            pl.BlockSpec(memory_space=pl.ANY),
        ],
        out_specs=pl.BlockSpec((tm, D_MODEL), row),
        out_shape=jax.ShapeDtypeStruct((t, D_MODEL), F32),
        scratch_shapes=[pltpu.VMEM((2, TOP_K * tm * ROW_SLABS, LANES), F32),
                        pltpu.VMEM((TOP_K * tm, D_MODEL), BF16),
                        pltpu.SemaphoreType.DMA((2,))],
        compiler_params=_cparams(("arbitrary",)),
        name="moe_combine",
    )(cnt, seg_off, seg_dst, cnt, seg_off, seg_dst, x2, pos, gates, mod, y)


def _plan(cnt, tm_moe, n_tiles_moe):
    c = cnt[:, 0, :]
    totals = jnp.sum(c, axis=0)
    padded = (totals + tm_moe - 1) // tm_moe * tm_moe
    pad_end = jnp.cumsum(padded)
    pad_start = pad_end - padded
    seg_dst = pad_start[None, :] + jnp.cumsum(c, axis=0) - c
    seg_off = jnp.cumsum(c, axis=1) - c
    tile_start = jnp.arange(n_tiles_moe, dtype=I32) * tm_moe
    block_e = jnp.minimum(jnp.sum(tile_start[:, None] >= pad_end[None, :N_EXPERTS], axis=1),
                          N_EXPERTS - 1).astype(I32)
    n_used = (pad_end[N_EXPERTS - 1] // tm_moe).astype(I32).reshape(1)
    shape = cnt.shape
    return seg_off.astype(I32).reshape(shape), seg_dst.astype(I32).reshape(shape), block_e, n_used


def _pick(n, pref):
    for c in pref:
        if n % c == 0:
            return c
    raise ValueError(f"no tile size in {pref} divides {n}")


def _trunk(x, mods, prm):
    batch, seq, _ = x.shape
    t = batch * seq
    assert seq % RET_CHUNK == 0 and seq % ATT_BLOCK == 0
    tm_in = _pick(seq, (512, 256, 128))
    tm_route = _pick(seq, (256, 128))
    tm_moe = 512 if t * TOP_K >= 65536 * 4 else (256 if t * TOP_K >= 16384 else 128)
    n_tiles_moe = (t * TOP_K) // tm_moe + N_EXPERTS
    x2 = x.reshape(t, D_MODEL)
    xs = jnp.zeros((n_tiles_moe * tm_moe * ROW_SLABS, LANES), F32)
    tri = (jnp.arange(tm_route)[:, None] > jnp.arange(tm_route)[None, :]).astype(BF16)
    upper = (jnp.arange(LANES)[:, None] < jnp.arange(LANES)[None, :]).astype(BF16)
    depth = prm["w_in"].shape[0]
    for l in range(depth):
        mod = mods[l]
        qr, kr, vr, sg, qa, ka, va = _in_call(
            x2, mod, prm["norm_mix_g"][l], prm["w_in"][l], prm["g128"],
            prm["q_g"][l], prm["k_g"][l], seq, tm_in)
        o_r = _ret_call(qr, kr, vr, sg, prm["lgf"][l], prm["lgb"][l], prm["g128"], batch, seq)
        o_a = _attn_call(qa, ka, va, prm["sink"][l], prm["bias"], batch, seq)
        x2, hf, pos, post, gates, cnt = _route_call(
            x2, o_r, o_a, mod, prm["w_out"][l], prm["norm_ffn_g"][l],
            prm["rw_hi"][l], prm["rw_lo"][l], prm["rb"][l], tri, upper, seq, tm_route)
        seg_off, seg_dst, block_e, n_used = _plan(cnt, tm_moe, n_tiles_moe)
        xs = _dispatch_call(cnt, seg_off, seg_dst, post, hf, xs, tm_route)
        y = _moe_call(block_e, n_used, xs, prm["w_g"][l], prm["w_l"][l], prm["b_g"][l],
                      prm["b_l"][l], prm["w_dn"][l], prm["b_dn"][l], tm_moe)
        x2 = _combine_call(cnt, seg_off, seg_dst, x2, pos, gates, mod, y, seq, tm_route)
    return x2.reshape(batch, seq, D_MODEL)


def _prepare(ada_w, ada_b, norm_mix_g, w_in, ret_log_gamma_f, ret_log_gamma_b, q_norm_g,
             k_norm_g, attn_sink, w_out, norm_ffn_g, router_w, router_b, w_gu, b_gu, w_dn, b_dn):
    depth = w_in.shape[0]
    pad_e = LANES - N_EXPERTS
    rw = jnp.pad(router_w, ((0, 0), (0, 0), (0, pad_e)))
    rw_hi = rw.astype(BF16)
    rw_lo = (rw - rw_hi.astype(F32)).astype(BF16)
    rb = jnp.pad(router_b, ((0, 0), (0, pad_e)), constant_values=NEG_INF).reshape(depth, 1, LANES)
    lane = jnp.arange(LANES)
    g128 = jnp.where((lane[:, None] // HEAD_DIM) == (lane[None, :] // HEAD_DIM),
                     1.0 / HEAD_DIM, 0.0).astype(BF16)
    w_g, w_l = _deint_call(w_gu)
    return dict(
        norm_mix_g=norm_mix_g.reshape(depth, 1, D_MODEL),
        norm_ffn_g=norm_ffn_g.reshape(depth, 1, D_MODEL),
        w_in=w_in.astype(BF16),
        w_out=w_out.astype(BF16),
        lgf=ret_log_gamma_f.astype(F32),
        lgb=ret_log_gamma_b.astype(F32),
        q_g=jnp.tile(q_norm_g, (1, LANES // HEAD_DIM)).reshape(depth, 1, LANES),
        k_g=jnp.tile(k_norm_g, (1, LANES // HEAD_DIM)).reshape(depth, 1, LANES),
        sink=attn_sink.astype(F32),
        rw_hi=rw_hi, rw_lo=rw_lo, rb=rb,
        w_g=w_g,
        w_l=w_l,
        b_g=b_gu[..., 0::2].reshape(depth, N_EXPERTS, 1, D_FF),
        b_l=b_gu[..., 1::2].reshape(depth, N_EXPERTS, 1, D_FF),
        w_dn=w_dn.astype(BF16),
        b_dn=b_dn.reshape(depth, N_EXPERTS, 1, D_MODEL),
        g128=g128,
        bias=_attn_bias(),
    )


def kernel(x_prompt, x_sample, c_prompt, c_sample, ada_w, ada_b, norm_mix_g, w_in, ret_log_gamma_f, ret_log_gamma_b, q_norm_g, k_norm_g, attn_sink, w_out, norm_ffn_g, router_w, router_b, w_gu, b_gu, w_dn, b_dn):
    prm = _prepare(ada_w, ada_b, norm_mix_g, w_in, ret_log_gamma_f, ret_log_gamma_b, q_norm_g,
                   k_norm_g, attn_sink, w_out, norm_ffn_g, router_w, router_b, w_gu, b_gu,
                   w_dn, b_dn)
    bp, bs = c_prompt.shape[0], c_sample.shape[0]
    n_c = bp + bs
    n_pad = -n_c % SUBLANES
    c_all = jnp.concatenate([c_prompt, c_sample, jnp.zeros((n_pad, D_MODEL), F32)], axis=0)
    mods = _ada_call(c_all, ada_w, ada_b)
    depth = ada_w.shape[0]
    mods_p = mods[:, :bp].reshape(depth, bp, 1, 6 * D_MODEL)
    mods_s = mods[:, bp:n_c].reshape(depth, bs, 1, 6 * D_MODEL)
    y_prompt = _trunk(x_prompt, mods_p, prm)
    y_sample = _trunk(x_sample, mods_s, prm)
    return (y_prompt, y_sample)
```

```python
import functools

import jax
import jax.numpy as jnp
from jax import lax
from jax.experimental import pallas as pl
from jax.experimental.pallas import tpu as pltpu

F32 = jnp.float32
BF16 = jnp.bfloat16
I32 = jnp.int32

D_MODEL = 1024
RET_HEADS = 8
HEAD_DIM = 64
RET_WIDTH = RET_HEADS * HEAD_DIM
RET_CHUNK = 128
ATT_HEADS = 8
ATT_KV_HEADS = 2
ATT_WIDTH = ATT_HEADS * HEAD_DIM
ATT_KV_WIDTH = ATT_KV_HEADS * HEAD_DIM
WINDOW = 128
ATT_BLOCK = 128
IN_WIDTH = 4 * RET_WIDTH + ATT_WIDTH + 2 * ATT_KV_WIDTH
N_EXPERTS = 32
TOP_K = 4
D_FF = D_MODEL
SWIGLU_LIMIT = 7.0
SWIGLU_ALPHA = 1.702
RMS_EPS = 1e-6
NEG_INF = -1e30

LANES = 128
SUBLANES = 8
ROW_SLABS = D_MODEL // LANES
VMEM_LIMIT = 56 * 1024 * 1024


def _cparams(sem):
    return pltpu.CompilerParams(dimension_semantics=sem, vmem_limit_bytes=VMEM_LIMIT)


def _sigmoid(x):
    return 1.0 / (1.0 + jnp.exp(-x))


def _split_bf16(x):
    hi = x.astype(BF16)
    lo = (x - hi.astype(F32)).astype(BF16)
    return hi, lo


def _dot(a, b):
    return jnp.dot(a, b, preferred_element_type=F32)


def _dot_nt(a, b):
    return lax.dot_general(a, b, (((1,), (1,)), ((), ())), preferred_element_type=F32)


def _group_mean(x2, g_ref):
    hi, lo = _split_bf16(x2)
    g = g_ref[...]
    return _dot(hi, g) + _dot(lo, g)


def _ada_kernel(c_ref, w_ref, b_ref, o_ref):
    c = c_ref[...]
    a = c * _sigmoid(c)
    a_hi, a_lo = _split_bf16(a)
    w_hi, w_lo = _split_bf16(w_ref[0])
    acc = _dot(a_hi, w_hi) + _dot(a_lo, w_hi) + _dot(a_hi, w_lo)
    o_ref[0] = acc + b_ref[0]


def _ada_call(c, ada_w, ada_b):
    depth = ada_w.shape[0]
    bp = c.shape[0]
    tn = 1536
    n_out = ada_w.shape[2]
    return pl.pallas_call(
        _ada_kernel,
        grid=(depth, n_out // tn),
        in_specs=[
            pl.BlockSpec((bp, D_MODEL), lambda l, j: (0, 0)),
            pl.BlockSpec((1, D_MODEL, tn), lambda l, j: (l, 0, j)),
            pl.BlockSpec((1, 1, tn), lambda l, j: (l, 0, j)),
        ],
        out_specs=pl.BlockSpec((1, bp, tn), lambda l, j: (l, 0, j)),
        out_shape=jax.ShapeDtypeStruct((depth, bp, n_out), F32),
        compiler_params=_cparams(("arbitrary", "arbitrary")),
        name="ada_mod",
    )(c, ada_w, ada_b.reshape(depth, 1, n_out))


DEINT_COLS = 256


def _deint_kernel(w_ref, p_ref, g_ref, l_ref):
    perm = p_ref[...]
    half = DEINT_COLS // 2
    for j in range(w_ref.shape[2] // DEINT_COLS):
        w = w_ref[0, :, j * DEINT_COLS:(j + 1) * DEINT_COLS].astype(BF16)
        r = _dot(w, perm)
        g_ref[0, :, j * half:(j + 1) * half] = r[:, :half].astype(BF16)
        l_ref[0, :, j * half:(j + 1) * half] = r[:, half:].astype(BF16)


def _deint_call(w_gu):
    depth, n_e, d_in, d_out2 = w_gu.shape
    w = w_gu.reshape(depth * n_e, d_in, d_out2)
    tk = 512
    src = jnp.arange(DEINT_COLS)
    dst = jnp.where(src % 2 == 0, src // 2, DEINT_COLS // 2 + src // 2)
    perm = (dst[:, None] == jnp.arange(DEINT_COLS)[None, :]).astype(BF16)
    out = jax.ShapeDtypeStruct((depth * n_e, d_in, d_out2 // 2), BF16)
    blk = lambda e, i: (e, i, 0)
    w_g, w_l = pl.pallas_call(
        _deint_kernel,
        grid=(depth * n_e, d_in // tk),
        in_specs=[pl.BlockSpec((1, tk, d_out2), blk),
                  pl.BlockSpec((DEINT_COLS, DEINT_COLS), lambda e, i: (0, 0))],
        out_specs=[pl.BlockSpec((1, tk, d_out2 // 2), blk)] * 2,
        out_shape=[out, out],
        compiler_params=_cparams(("parallel", "parallel")),
        name="expert_weight_split",
    )(w, perm)
    shape = (depth, n_e, d_in, d_out2 // 2)
    return w_g.reshape(shape), w_l.reshape(shape)


def _modulate(x, gain, shift, scale):
    ms = jnp.mean(x * x, axis=-1, keepdims=True)
    xn = x * lax.rsqrt(ms + RMS_EPS)
    return xn * gain * (1.0 + scale) + shift


def _in_kernel(x_ref, mod_ref, gain_ref, w_ref, g_ref, qg_ref, kg_ref,
               qr_ref, kr_ref, vr_ref, sg_ref, qa_ref, ka_ref, va_ref):
    mod = mod_ref[0]
    h = _modulate(x_ref[...], gain_ref[...], mod[:, 0:D_MODEL], mod[:, D_MODEL:2 * D_MODEL])
    proj = _dot(h.astype(BF16), w_ref[...])
    w = RET_WIDTH
    qr_ref[...] = proj[:, 0:w].astype(BF16)
    kr_ref[...] = (proj[:, w:2 * w] * (HEAD_DIM ** -0.5)).astype(BF16)
    vr_ref[...] = proj[:, 2 * w:3 * w].astype(BF16)
    gr = proj[:, 3 * w:4 * w]
    sg_ref[...] = (gr * _sigmoid(gr)).astype(BF16)
    base = 4 * w
    qg = qg_ref[...] * (HEAD_DIM ** -0.5 * LOG2E)
    for s in range(ATT_WIDTH // LANES):
        q = proj[:, base + s * LANES: base + (s + 1) * LANES]
        msq = _group_mean(q * q, g_ref)
        qa_ref[:, s * LANES:(s + 1) * LANES] = (q * lax.rsqrt(msq + RMS_EPS) * qg).astype(BF16)
    kb = base + ATT_WIDTH
    k = proj[:, kb:kb + LANES]
    msk = _group_mean(k * k, g_ref)
    kn = k * lax.rsqrt(msk + RMS_EPS) * kg_ref[...]
    v = proj[:, kb + LANES:kb + 2 * LANES]
    lane = lax.broadcasted_iota(I32, kn.shape, 1)
    first = lane < HEAD_DIM
    for src, dst in ((kn, ka_ref), (v, va_ref)):
        rot = pltpu.roll(src, HEAD_DIM, 1)
        dst[:, 0:LANES] = jnp.where(first, src, rot).astype(BF16)
        dst[:, LANES:2 * LANES] = jnp.where(first, rot, src).astype(BF16)


def _in_call(x2, mod, gain, w_in, g128, qg, kg, seq, tm):
    t = x2.shape[0]
    row = lambda i: (i, 0)
    const = lambda i: (0, 0)
    outs = [
        jax.ShapeDtypeStruct((t, RET_WIDTH), BF16),
        jax.ShapeDtypeStruct((t, RET_WIDTH), BF16),
        jax.ShapeDtypeStruct((t, RET_WIDTH), BF16),
        jax.ShapeDtypeStruct((t, RET_WIDTH), BF16),
        jax.ShapeDtypeStruct((t, ATT_WIDTH), BF16),
        jax.ShapeDtypeStruct((t, 2 * LANES), BF16),
        jax.ShapeDtypeStruct((t, 2 * LANES), BF16),
    ]
    return pl.pallas_call(
        _in_kernel,
        grid=(t // tm,),
        in_specs=[
            pl.BlockSpec((tm, D_MODEL), row),
            pl.BlockSpec((1, 1, 6 * D_MODEL), lambda i: ((i * tm) // seq, 0, 0)),
            pl.BlockSpec((1, D_MODEL), const),
            pl.BlockSpec((D_MODEL, IN_WIDTH), const),
            pl.BlockSpec((LANES, LANES), const),
            pl.BlockSpec((1, LANES), const),
            pl.BlockSpec((1, LANES), const),
        ],
        out_specs=[pl.BlockSpec((tm, o.shape[1]), row) for o in outs],
        out_shape=outs,
        compiler_params=_cparams(("parallel",)),
        name="in_proj",
    )(x2, mod, gain, w_in, g128, qg, kg)


def _ret_kernel(lgf_ref, lgb_ref, q_ref, k_ref, v_ref, sg_ref, g_ref, o_ref,
                uf_ref, ub_ref, sf_ref, sb_ref, s_ref, acc_ref):
    p = pl.program_id(1)
    c = RET_CHUNK
    n_chunks = q_ref.shape[0] // c
    lf0, lf1 = lgf_ref[2 * p], lgf_ref[2 * p + 1]
    lb0, lb1 = lgb_ref[2 * p], lgb_ref[2 * p + 1]
    row = lax.broadcasted_iota(I32, (c, LANES), 0)
    lane = lax.broadcasted_iota(I32, (c, LANES), 1)
    pos = row.astype(F32)
    first = lane < HEAD_DIM
    lf_lane = jnp.where(first, lf0, lf1)
    lb_lane = jnp.where(first, lb0, lb1)
    qd_f = jnp.exp(lf_lane * (pos + 1.0))
    kd_f = jnp.exp(lf_lane * (c - 1.0 - pos))
    qd_b = jnp.exp(lb_lane * (c - pos))
    kd_b = jnp.exp(lb_lane * pos)
    cd_f = jnp.exp(lf_lane * float(c))
    cd_b = jnp.exp(lb_lane * float(c))
    same_head = (row < HEAD_DIM) == first
    row2 = lax.broadcasted_iota(I32, (c, 2 * c), 0)
    col2 = lax.broadcasted_iota(I32, (c, 2 * c), 1)
    head1 = col2 >= c
    diff = (row2 - jnp.where(head1, col2 - c, col2)).astype(F32)
    lf_c = jnp.where(head1, lf1, lf0)
    lb_c = jnp.where(head1, lb1, lb0)
    decay = jnp.where(diff >= 0, jnp.exp(lf_c * jnp.maximum(diff, 0.0)),
                      jnp.exp(lb_c * jnp.maximum(-diff, 0.0)))

    unroll = _pick(n_chunks, (4, 2, 1))

    def updates(g, carry):
        for u in range(unroll):
            n = g * unroll + u
            off = pl.multiple_of(n * c, c)
            k = k_ref[pl.ds(off, c), :].astype(F32)
            vt = v_ref[pl.ds(off, c), :].astype(F32).T.astype(BF16)
            uf_ref[n] = jnp.where(same_head, _dot(vt, (k * kd_f).astype(BF16)), 0.0)
            ub_ref[n] = jnp.where(same_head, _dot(vt, (k * kd_b).astype(BF16)), 0.0)
        return carry

    lax.fori_loop(0, n_chunks // unroll, updates, 0)

    def scan_f(n, state):
        sf_ref[n] = state.astype(BF16)
        return state * cd_f + uf_ref[n]

    def scan_b(i, state):
        n = n_chunks - 1 - i
        sb_ref[n] = state.astype(BF16)
        return state * cd_b + ub_ref[n]

    zero_state = jnp.zeros((LANES, LANES), F32)
    lax.fori_loop(0, n_chunks, scan_f, zero_state)
    lax.fori_loop(0, n_chunks, scan_b, zero_state)

    group = s_ref.shape[0]
    zero = jnp.zeros((c, LANES), BF16)

    def split_heads(x):
        return jnp.concatenate([jnp.where(first, x, zero), jnp.where(first, zero, x)], axis=0)

    def outputs(g, carry):
        offs = [pl.multiple_of((g * group + u) * c, c) for u in range(group)]
        for u in range(group):
            s_ref[u] = _dot_nt(q_ref[pl.ds(offs[u], c), :],
                               split_heads(k_ref[pl.ds(offs[u], c), :]))
        for u in range(group):
            n = g * group + u
            q = q_ref[pl.ds(offs[u], c), :].astype(F32)
            acc_ref[u] = (_dot((s_ref[u] * decay).astype(BF16), split_heads(v_ref[pl.ds(offs[u], c), :]))
                          + _dot_nt((q * qd_f).astype(BF16), sf_ref[n])
                          + _dot_nt((q * qd_b).astype(BF16), sb_ref[n]))
        for u in range(group):
            o = acc_ref[u]
            ms = _group_mean(o * o, g_ref)
            on = o * lax.rsqrt(ms + RMS_EPS)
            o_ref[pl.ds(offs[u], c), :] = (on * sg_ref[pl.ds(offs[u], c), :].astype(F32)).astype(BF16)
        return carry

    lax.fori_loop(0, n_chunks // group, outputs, 0)


def _ret_call(qr, kr, vr, sg, lgf, lgb, g128, batch, seq):
    t = qr.shape[0]
    blk = lambda b, p: (b, p)
    smem = pl.BlockSpec(memory_space=pltpu.SMEM)
    n_chunks = seq // RET_CHUNK
    group = _pick(n_chunks, (8, 4, 2, 1))
    return pl.pallas_call(
        _ret_kernel,
        grid=(batch, RET_HEADS // 2),
        in_specs=[smem, smem] + [pl.BlockSpec((seq, LANES), blk)] * 4
                 + [pl.BlockSpec((LANES, LANES), lambda b, p: (0, 0))],
        out_specs=pl.BlockSpec((seq, LANES), blk),
        out_shape=jax.ShapeDtypeStruct((t, RET_WIDTH), BF16),
        scratch_shapes=[pltpu.VMEM((n_chunks, LANES, LANES), F32),
                        pltpu.VMEM((n_chunks, LANES, LANES), F32),
                        pltpu.VMEM((n_chunks, LANES, LANES), BF16),
                        pltpu.VMEM((n_chunks, LANES, LANES), BF16),
                        pltpu.VMEM((group, RET_CHUNK, 2 * RET_CHUNK), F32),
                        pltpu.VMEM((group, RET_CHUNK, LANES), F32)],
        compiler_params=_cparams(("parallel", "parallel")),
        name="retention",
    )(lgf, lgb, qr, kr, vr, sg, g128)


ATT_SUB = 2


ATT_STRIP = 32


def _attn_kernel(sink_ref, q_ref, kp_ref, kc_ref, kn_ref, vp_ref, vc_ref, vn_ref,
                 bias_a_ref, bias_b_ref, o_ref, kcat_ref, vcat_ref, s_ref, p_ref, inv_ref):
    blk = ATT_BLOCK
    rows = (ATT_SUB + 2) * blk
    first = lax.broadcasted_iota(I32, (rows, LANES), 1) < HEAD_DIM
    out_first = lax.broadcasted_iota(I32, (ATT_STRIP, LANES), 1) < HEAD_DIM
    kk = jnp.concatenate([kp_ref[...], kc_ref[...], kn_ref[...]], axis=0)
    vv = jnp.concatenate([vp_ref[...], vc_ref[...], vn_ref[...]], axis=0)
    zero = jnp.zeros((rows, LANES), BF16)
    pairs_per_kv = ATT_HEADS // ATT_KV_HEADS // 2
    combos = []
    for kv in range(ATT_KV_HEADS):
        kd = kk[:, kv * LANES:(kv + 1) * LANES]
        vd = vv[:, kv * LANES:(kv + 1) * LANES]
        k_lo, k_hi = jnp.where(first, kd, zero), jnp.where(first, zero, kd)
        v_lo, v_hi = jnp.where(first, vd, zero), jnp.where(first, zero, vd)
        for sub in range(ATT_SUB):
            band = slice(sub * blk, (sub + 3) * blk)
            g = kv * ATT_SUB + sub
            kcat_ref[g] = jnp.concatenate([k_lo[band], k_hi[band]], axis=0)
            vcat_ref[g] = jnp.concatenate([v_lo[band], v_hi[band]], axis=0)
            bias_ref = bias_a_ref if sub == 0 else bias_b_ref
            qrows = slice(sub * blk, (sub + 1) * blk)
            for pp in range(pairs_per_kv):
                hp = kv * pairs_per_kv + pp
                combos.append((len(combos), qrows, hp, bias_ref, g))
    for c, qrows, hp, bias_ref, g in combos:
        s_ref[c] = _dot_nt(q_ref[qrows, hp * LANES:(hp + 1) * LANES], kcat_ref[g])
    for c, qrows, hp, bias_ref, g in combos:
        for r0 in range(0, blk, ATT_STRIP):
            strip = slice(r0, r0 + ATT_STRIP)
            inv = []
            for hh in range(2):
                h = 2 * hp + hh
                cols = slice(hh * 3 * blk, (hh + 1) * 3 * blk)
                sh = s_ref[c, strip, cols] + bias_ref[0, h, strip, :]
                sink = sink_ref[h]
                m = jnp.maximum(jnp.max(sh, axis=-1, keepdims=True), sink)
                e = jnp.exp2(sh - m)
                den = jnp.sum(e, axis=-1, keepdims=True) + jnp.exp2(sink - m)
                p_ref[c, strip, cols] = e.astype(BF16)
                inv.append(1.0 / den)
            inv_ref[c, strip, :] = jnp.where(out_first, inv[0], inv[1])
    for c, qrows, hp, bias_ref, g in combos:
        o = _dot(p_ref[c], vcat_ref[g]) * inv_ref[c]
        o_ref[qrows, hp * LANES:(hp + 1) * LANES] = o.astype(BF16)


LOG2E = 1.4426950408889634


def _attn_bias():
    blk = ATT_BLOCK
    i = jnp.arange(blk, dtype=F32)[:, None]
    jj = jnp.arange(3 * blk, dtype=F32)[None, :]
    rel = jnp.abs(i + blk - jj)
    slopes = 2.0 ** (-8.0 * (jnp.arange(ATT_HEADS, dtype=F32) + 1.0) / ATT_HEADS)
    base = jnp.where(rel <= WINDOW, -slopes[:, None, None] * rel * LOG2E, NEG_INF)
    no_prev = jnp.where(jj < blk, NEG_INF, base)
    no_next = jnp.where(jj >= 2 * blk, NEG_INF, base)
    return jnp.stack([no_prev, base, no_next])


def _attn_call(qa, ka, va, sink, bias, batch, seq):
    t = qa.shape[0]
    blk = ATT_BLOCK
    assert seq % (ATT_SUB * blk) == 0
    nb = seq // blk
    ns = nb // ATT_SUB
    cur = lambda b, j: (b * ns + j, 0)
    prev = lambda b, j: (b * nb + jnp.maximum(ATT_SUB * j - 1, 0), 0)
    nxt = lambda b, j: (b * nb + jnp.minimum(ATT_SUB * (j + 1), nb - 1), 0)
    kv_specs = [pl.BlockSpec((blk, 2 * LANES), prev),
                pl.BlockSpec((ATT_SUB * blk, 2 * LANES), cur),
                pl.BlockSpec((blk, 2 * LANES), nxt)]
    bias_block = (1, ATT_HEADS, blk, 3 * blk)
    n_combo = ATT_SUB * ATT_HEADS // 2
    return pl.pallas_call(
        _attn_kernel,
        grid=(batch, ns),
        in_specs=[pl.BlockSpec(memory_space=pltpu.SMEM),
                  pl.BlockSpec((ATT_SUB * blk, ATT_WIDTH), cur)] + kv_specs + kv_specs
                 + [pl.BlockSpec(bias_block, lambda b, j: (jnp.where(j == 0, 0, 1), 0, 0, 0)),
                    pl.BlockSpec(bias_block, lambda b, j: (jnp.where(j == ns - 1, 2, 1), 0, 0, 0))],
        out_specs=pl.BlockSpec((ATT_SUB * blk, ATT_WIDTH), cur),
        out_shape=jax.ShapeDtypeStruct((t, ATT_WIDTH), BF16),
        scratch_shapes=[pltpu.VMEM((ATT_KV_HEADS * ATT_SUB, 6 * blk, LANES), BF16),
                        pltpu.VMEM((ATT_KV_HEADS * ATT_SUB, 6 * blk, LANES), BF16),
                        pltpu.VMEM((n_combo, blk, 6 * blk), F32),
                        pltpu.VMEM((n_combo, blk, 6 * blk), BF16),
                        pltpu.VMEM((n_combo, blk, LANES), F32)],
        compiler_params=_cparams(("parallel", "arbitrary")),
        name="window_attn",
    )(sink, qa, ka, ka, ka, va, va, va, bias, bias)


def _route_kernel(x_ref, or_ref, oa_ref, mod_ref, wo_ref, gain_ref, rwh_ref, rwl_ref,
                  rb_ref, tri_ref, upper_ref, xo_ref, hf_ref, pos_ref, post_ref, gate_ref, cnt_ref,
                  lg_ref, sel_ref, idx_ref, bef_ref, posf_ref):
    sub = tri_ref.shape[0]
    for h in range(x_ref.shape[0] // sub):
        rows = slice(h * sub, (h + 1) * sub)
        _route_tile(x_ref.at[rows], or_ref.at[rows], oa_ref.at[rows], mod_ref, wo_ref, gain_ref,
                    rwh_ref, rwl_ref, rb_ref, tri_ref, upper_ref, xo_ref.at[rows], hf_ref.at[rows],
                    pos_ref.at[rows], post_ref.at[h], gate_ref.at[rows], cnt_ref.at[h],
                    lg_ref.at[h], sel_ref.at[h], idx_ref.at[h], bef_ref.at[h], posf_ref.at[h])


ROUTE_STRIP = 64


def _route_tile(x_ref, or_ref, oa_ref, mod_ref, wo_ref, gain_ref, rwh_ref, rwl_ref,
                rb_ref, tri_ref, upper_ref, xo_ref, hf_ref, pos_ref, post_ref, gate_ref, cnt_ref,
                lg_ref, sel_ref, idx_ref, bef_ref, posf_ref):
    mod = mod_ref[0]
    d = D_MODEL
    sub = lg_ref.shape[0]
    mix = _dot(or_ref[...], wo_ref[0:RET_WIDTH, :]) + _dot(oa_ref[...], wo_ref[RET_WIDTH:, :])
    x1 = x_ref[...] + mod[:, 2 * d:3 * d] * mix
    xo_ref[...] = x1
    hf = _modulate(x1, gain_ref[...], mod[:, 3 * d:4 * d], mod[:, 4 * d:5 * d])
    h_hi, h_lo = _split_bf16(hf)
    hf_ref[...] = h_hi
    rwh = rwh_ref[...]
    lg_ref[...] = _dot(h_hi, rwh) + _dot(h_lo, rwh) + _dot(h_hi, rwl_ref[...]) + rb_ref[...]
    lane = lax.broadcasted_iota(I32, (ROUTE_STRIP, LANES), 1)
    lane_f = lane.astype(F32)
    strips =[slice(r0, r0 + ROUTE_STRIP) for r0 in range(0, sub, ROUTE_STRIP)]
    counts = jnp.zeros((1, LANES), F32)
    for strip in strips:
        work = lg_ref[strip, :]
        sel = jnp.zeros(work.shape, F32)
        idx_m = jnp.zeros(work.shape, F32)
        vals = []
        for k in range(TOP_K):
            m = jnp.max(work, axis=-1, keepdims=True)
            idx = jnp.min(jnp.where(work == m, lane_f, float(LANES)), axis=-1, keepdims=True)
            hot = lane_f == idx
            work = jnp.where(hot, -jnp.inf, work)
            sel = sel + jnp.where(hot, 1.0, 0.0)
            idx_m = jnp.where(lane == k, idx, idx_m)
            vals.append(m)
        exps = [jnp.exp(v - vals[0]) for v in vals]
        den = exps[0] + exps[1] + exps[2] + exps[3]
        gates = jnp.zeros(work.shape, F32)
        for k in range(TOP_K):
            gates = jnp.where(lane == k, exps[k] / den, gates)
        gate_ref[strip, :] = gates
        idx_ref[strip, :] = idx_m
        sel_ref[strip, :] = sel.astype(BF16)
        counts = counts + jnp.sum(sel, axis=0, keepdims=True)
    bef_ref[...] = _dot(tri_ref[...], sel_ref[...])
    seg_off = _dot(jnp.broadcast_to(counts, (SUBLANES, LANES)).astype(BF16), upper_ref[...])[0:1]
    for strip in strips:
        slot = bef_ref[strip, :] + seg_off
        idx_m = idx_ref[strip, :]
        pos = jnp.zeros(slot.shape, F32)
        for k in range(TOP_K):
            hot = lane_f == idx_m[:, k:k + 1]
            pk = jnp.sum(jnp.where(hot, slot, 0.0), axis=-1, keepdims=True)
            pos = jnp.where(lane == k, pk, pos)
        posf_ref[strip, :] = pos
        pos_ref[strip, :] = pos.astype(I32)
    post_ref[...] = posf_ref[...].T[0:SUBLANES, :].astype(I32)
    cnt_ref[...] = counts.astype(I32)


def _route_call(x2, o_r, o_a, mod, w_out, gain, rw_hi, rw_lo, rb, tri, upper, seq, sub, tm):
    t = x2.shape[0]
    nt = t // tm
    n_sub = tm // sub
    row = lambda i: (i, 0)
    const = lambda i: (0, 0)
    outs = [
        jax.ShapeDtypeStruct((t, D_MODEL), F32),
        jax.ShapeDtypeStruct((t, D_MODEL), BF16),
        jax.ShapeDtypeStruct((t, LANES), I32),
        jax.ShapeDtypeStruct((t // sub, SUBLANES, sub), I32),
        jax.ShapeDtypeStruct((t, LANES), F32),
        jax.ShapeDtypeStruct((t // sub, 1, LANES), I32),
    ]
    out_specs = [
        pl.BlockSpec((tm, D_MODEL), row),
        pl.BlockSpec((tm, D_MODEL), row),
        pl.BlockSpec((tm, LANES), row),
        pl.BlockSpec((n_sub, SUBLANES, sub), lambda i: (i, 0, 0)),
        pl.BlockSpec((tm, LANES), row),
        pl.BlockSpec((n_sub, 1, LANES), lambda i: (i, 0, 0)),
    ]
    return pl.pallas_call(
        _route_kernel,
        grid=(nt,),
        in_specs=[
            pl.BlockSpec((tm, D_MODEL), row),
            pl.BlockSpec((tm, RET_WIDTH), row),
            pl.BlockSpec((tm, ATT_WIDTH), row),
            pl.BlockSpec((1, 1, 6 * D_MODEL), lambda i: ((i * tm) // seq, 0, 0)),
            pl.BlockSpec((D_MODEL, D_MODEL), const),
            pl.BlockSpec((1, D_MODEL), const),
            pl.BlockSpec((D_MODEL, LANES), const),
            pl.BlockSpec((D_MODEL, LANES), const),
            pl.BlockSpec((1, LANES), const),
            pl.BlockSpec((sub, sub), const),
            pl.BlockSpec((LANES, LANES), const),
        ],
        out_specs=out_specs,
        out_shape=outs,
        scratch_shapes=[pltpu.VMEM((n_sub, sub, LANES), F32),
                        pltpu.VMEM((n_sub, sub, LANES), BF16),
                        pltpu.VMEM((n_sub, sub, LANES), F32),
                        pltpu.VMEM((n_sub, sub, LANES), F32),
                        pltpu.VMEM((n_sub, sub, LANES), F32)],
        compiler_params=_cparams(("parallel",)),
        name="out_proj_route",
    )(x2, o_r, o_a, mod, w_out, gain, rw_hi, rw_lo, rb, tri, upper)


def _rows(ref, row, n):
    return ref.at[pl.ds(pl.multiple_of(row * ROW_SLABS, SUBLANES), n * ROW_SLABS), :]


def _segment_copies(cnt_ref, off_ref, dst_ref, tm, make_copy):
    def per_expert(e, carry):
        n = cnt_ref[0, 0, e]
        off = off_ref[0, 0, e]
        dst = dst_ref[0, 0, e]
        done = jnp.int32(0)
        piece = tm
        while piece >= 1:
            @pl.when((n & piece) != 0)
            def _(piece=piece, done=done):
                make_copy(off + done, dst + done, piece).start()
            done = done + (n & piece)
            piece //= 2
        return carry

    lax.fori_loop(0, N_EXPERTS, per_expert, 0)


def _dispatch_kernel(cnt_ref, off_ref, dst_ref, post_ref, hf_ref, xs_in_ref, xs_ref, stage_ref, sems):
    del xs_in_ref
    i = pl.program_id(0)
    tm = hf_ref.shape[0]
    n_slots = TOP_K * tm
    slot = i % 2
    stage = stage_ref.at[slot]
    post = post_ref[0]
    j = lax.broadcasted_iota(I32, (n_slots, tm), 0)
    perm = sum(jnp.where(post[k:k + 1, :] == j, 1.0, 0.0) for k in range(TOP_K))
    srt = _dot(perm.astype(BF16), hf_ref[...])
    for c in range(ROW_SLABS):
        stage[pl.ds(c, n_slots, stride=ROW_SLABS), :] = srt[:, c * LANES:(c + 1) * LANES]
    _segment_copies(cnt_ref, off_ref, dst_ref, tm,
                    lambda o, d, n: pltpu.make_async_copy(_rows(stage, o, n), _rows(xs_ref, d, n),
                                                          sems.at[slot]))

    def wait_tile(s):
        pltpu.make_async_copy(stage_ref.at[s], _rows(xs_ref, 0, n_slots), sems.at[s]).wait()

    @pl.when(i > 0)
    def _():
        wait_tile(1 - slot)

    @pl.when(i == pl.num_programs(0) - 1)
    def _():
        wait_tile(slot)


def _seg_spec(index_map):
    return pl.BlockSpec((1, 1, LANES), index_map, memory_space=pltpu.SMEM)


def _dispatch_call(cnt, seg_off, seg_dst, post, hf, xs_prev, tm):
    t = hf.shape[0]
    tile = lambda i: (i, 0, 0)
    return pl.pallas_call(
        _dispatch_kernel,
        grid=(t // tm,),
        in_specs=[
            _seg_spec(tile), _seg_spec(tile), _seg_spec(tile),
            pl.BlockSpec((1, SUBLANES, tm), tile),
            pl.BlockSpec((tm, D_MODEL), lambda i: (i, 0)),
            pl.BlockSpec(memory_space=pl.ANY),
        ],
        out_specs=pl.BlockSpec(memory_space=pl.ANY),
        out_shape=jax.ShapeDtypeStruct(xs_prev.shape, F32),
        scratch_shapes=[pltpu.VMEM((2, TOP_K * tm * ROW_SLABS, LANES), F32),
                        pltpu.SemaphoreType.DMA((2,))],
        input_output_aliases={5: 0},
        compiler_params=_cparams(("arbitrary",)),
        name="moe_dispatch",
    )(cnt, seg_off, seg_dst, post, hf, xs_prev)


def _moe_kernel(be_ref, nu_ref, xs_ref, wg_ref, wl_ref, bg_ref, bl_ref, wd_ref, bd_ref,
                y_ref, xb_ref):
    del be_ref
    i = pl.program_id(0)
    tm = xb_ref.shape[0]

    @pl.when(i < nu_ref[0])
    def _():
        for c in range(ROW_SLABS):
            xb_ref[:, c * LANES:(c + 1) * LANES] = (
                xs_ref[pl.ds(c, tm, stride=ROW_SLABS), :].astype(BF16))
        x = xb_ref[...]
        g = _dot(x, wg_ref[0]) + bg_ref[0]
        lin = _dot(x, wl_ref[0]) + bl_ref[0]
        glu = jnp.minimum(g, SWIGLU_LIMIT)
        lin = jnp.clip(lin, -SWIGLU_LIMIT, SWIGLU_LIMIT)
        act = (lin + 1.0) * glu * _sigmoid(SWIGLU_ALPHA * glu)
        y = _dot(act.astype(BF16), wd_ref[0]) + bd_ref[0]
        for c in range(ROW_SLABS):
            y_ref[pl.ds(c, tm, stride=ROW_SLABS), :] = y[:, c * LANES:(c + 1) * LANES]

    @pl.when(i >= nu_ref[0])
    def _():
        y_ref[...] = jnp.zeros_like(y_ref)


def _moe_call(block_e, n_used, xs, wg, wl, bg, bl, wd, bd, tm):
    n_tiles = xs.shape[0] // (tm * ROW_SLABS)
    row = lambda i, be, nu: (i, 0)
    wsel = lambda i, be, nu: (be[i], 0, 0)
    grid_spec = pltpu.PrefetchScalarGridSpec(
        num_scalar_prefetch=2,
        grid=(n_tiles,),
        in_specs=[
            pl.BlockSpec((tm * ROW_SLABS, LANES), row),
            pl.BlockSpec((1, D_MODEL, D_FF), wsel),
            pl.BlockSpec((1, D_MODEL, D_FF), wsel),
            pl.BlockSpec((1, 1, D_FF), wsel),
            pl.BlockSpec((1, 1, D_FF), wsel),
            pl.BlockSpec((1, D_FF, D_MODEL), wsel),
            pl.BlockSpec((1, 1, D_MODEL), wsel),
        ],
        out_specs=pl.BlockSpec((tm * ROW_SLABS, LANES), row),
        scratch_shapes=[pltpu.VMEM((tm, D_MODEL), BF16)],
    )
    return pl.pallas_call(
        _moe_kernel,
        grid_spec=grid_spec,
        out_shape=jax.ShapeDtypeStruct(xs.shape, F32),
        compiler_params=_cparams(("arbitrary",)),
        name="moe_experts",
    )(block_e, n_used, xs, wg, wl, bg, bl, wd, bd)


def _combine_kernel(cnt_ref, off_ref, dst_ref, cnt_n_ref, off_n_ref, dst_n_ref,
                    x_ref, pos_ref, gate_ref, mod_ref, y_ref, o_ref, buf_ref, yb_ref, sems):
    i = pl.program_id(0)
    nt = pl.num_programs(0)
    tm = x_ref.shape[0]
    n_slots = TOP_K * tm
    slot = i % 2

    def fetch(tables, s):
        buf = buf_ref.at[s]
        _segment_copies(*tables, tm,
                        lambda o, d, n: pltpu.make_async_copy(_rows(y_ref, d, n), _rows(buf, o, n),
                                                              sems.at[s]))

    @pl.when(i == 0)
    def _():
        fetch((cnt_ref, off_ref, dst_ref), slot)

    @pl.when(i + 1 < nt)
    def _():
        fetch((cnt_n_ref, off_n_ref, dst_n_ref), 1 - slot)

    pltpu.make_async_copy(_rows(y_ref, 0, n_slots), buf_ref.at[slot], sems.at[slot]).wait()
    buf = buf_ref.at[slot]
    for c in range(ROW_SLABS):
        yb_ref[:, c * LANES:(c + 1) * LANES] = buf[pl.ds(c, n_slots, stride=ROW_SLABS), :].astype(BF16)
    pos = pos_ref[...]
    gates = gate_ref[...]
    j = lax.broadcasted_iota(I32, (tm, n_slots), 1)
    wsel = sum(jnp.where(pos[:, k:k + 1] == j, gates[:, k:k + 1], 0.0) for k in range(TOP_K))
    ffn = _dot(wsel.astype(BF16), yb_ref[...])
    o_ref[...] = x_ref[...] + mod_ref[0][:, 5 * D_MODEL:6 * D_MODEL] * ffn


def _combine_call(cnt, seg_off, seg_dst, x2, pos, gates, mod, y, seq, tm):
    t = x2.shape[0]
    nt = t // tm
    row = lambda i: (i, 0)
    tile = lambda i: (i, 0, 0)
    nxt = lambda i: (jnp.minimum(i + 1, nt - 1), 0, 0)
    return pl.pallas_call(
        _combine_kernel,
        grid=(nt,),
        in_specs=[
            _seg_spec(tile), _seg_spec(tile), _seg_spec(tile),
            _seg_spec(nxt), _seg_spec(nxt), _seg_spec(nxt),
            pl.BlockSpec((tm, D_MODEL), row),
            pl.BlockSpec((tm, LANES), row),
            pl.BlockSpec((tm, LANES), row),
            pl.BlockSpec((1, 1, 6 * D_MODEL), lambda i: ((i * tm) // seq, 0, 0)),
            pl.BlockSpec(memory_space=pl.ANY),
        ],
        out_specs=pl.BlockSpec((tm, D_MODEL), row),
        out_shape=jax.ShapeDtypeStruct((t, D_MODEL), F32),
        scratch_shapes=[pltpu.VMEM((2, TOP_K * tm * ROW_SLABS, LANES), F32),
                        pltpu.VMEM((TOP_K * tm, D_MODEL), BF16),
                        pltpu.SemaphoreType.DMA((2,))],
        compiler_params=_cparams(("arbitrary",)),
        name="moe_combine",
    )(cnt, seg_off, seg_dst, cnt, seg_off, seg_dst, x2, pos, gates, mod, y)


def _plan(cnt, tm_moe, n_tiles_moe):
    c = cnt[:, 0, :]
    totals = jnp.sum(c, axis=0)
    padded = (totals + tm_moe - 1) // tm_moe * tm_moe
    pad_end = jnp.cumsum(padded)
    pad_start = pad_end - padded
    seg_dst = pad_start[None, :] + jnp.cumsum(c, axis=0) - c
    seg_off = jnp.cumsum(c, axis=1) - c
    tile_start = jnp.arange(n_tiles_moe, dtype=I32) * tm_moe
    block_e = jnp.minimum(jnp.sum(tile_start[:, None] >= pad_end[None, :N_EXPERTS], axis=1),
                          N_EXPERTS - 1).astype(I32)
    n_used = (pad_end[N_EXPERTS - 1] // tm_moe).astype(I32).reshape(1)
    shape = cnt.shape
    return seg_off.astype(I32).reshape(shape), seg_dst.astype(I32).reshape(shape), block_e, n_used


def _pick(n, pref):
    for c in pref:
        if n % c == 0:
            return c
    raise ValueError(f"no tile size in {pref} divides {n}")


def _trunk(x, mods, prm):
    batch, seq, _ = x.shape
    t = batch * seq
    assert seq % RET_CHUNK == 0 and seq % ATT_BLOCK == 0
    tm_in = _pick(seq, (512, 256, 128))
    tm_route = _pick(seq, (256, 128))
    tm_step = _pick(seq, (2 * tm_route, tm_route))
    tm_moe = 512 if t * TOP_K >= 65536 * 4 else (256 if t * TOP_K >= 16384 else 128)
    n_tiles_moe = (t * TOP_K) // tm_moe + N_EXPERTS
    x2 = x.reshape(t, D_MODEL)
    xs = jnp.zeros((n_tiles_moe * tm_moe * ROW_SLABS, LANES), F32)
    tri = (jnp.arange(tm_route)[:, None] > jnp.arange(tm_route)[None, :]).astype(BF16)
    upper = (jnp.arange(LANES)[:, None] < jnp.arange(LANES)[None, :]).astype(BF16)
    depth = prm["w_in"].shape[0]
    for l in range(depth):
        mod = mods[l]
        qr, kr, vr, sg, qa, ka, va = _in_call(
            x2, mod, prm["norm_mix_g"][l], prm["w_in"][l], prm["g128"],
            prm["q_g"][l], prm["k_g"][l], seq, tm_in)
        o_r = _ret_call(qr, kr, vr, sg, prm["lgf"][l], prm["lgb"][l], prm["g128"], batch, seq)
        o_a = _attn_call(qa, ka, va, prm["sink"][l], prm["bias"], batch, seq)
        x2, hf, pos, post, gates, cnt = _route_call(
            x2, o_r, o_a, mod, prm["w_out"][l], prm["norm_ffn_g"][l],
            prm["rw_hi"][l], prm["rw_lo"][l], prm["rb"][l], tri, upper, seq, tm_route, tm_step)
        seg_off, seg_dst, block_e, n_used = _plan(cnt, tm_moe, n_tiles_moe)
        xs = _dispatch_call(cnt, seg_off, seg_dst, post, hf, xs, tm_route)
        y = _moe_call(block_e, n_used, xs, prm["w_g"][l], prm["w_l"][l], prm["b_g"][l],
                      prm["b_l"][l], prm["w_dn"][l], prm["b_dn"][l], tm_moe)
        x2 = _combine_call(cnt, seg_off, seg_dst, x2, pos, gates, mod, y, seq, tm_route)
    return x2.reshape(batch, seq, D_MODEL)


def _prepare(ada_w, ada_b, norm_mix_g, w_in, ret_log_gamma_f, ret_log_gamma_b, q_norm_g,
             k_norm_g, attn_sink, w_out, norm_ffn_g, router_w, router_b, w_gu, b_gu, w_dn, b_dn):
    depth = w_in.shape[0]
    pad_e = LANES - N_EXPERTS
    rw = jnp.pad(router_w, ((0, 0), (0, 0), (0, pad_e)))
    rw_hi = rw.astype(BF16)
    rw_lo = (rw - rw_hi.astype(F32)).astype(BF16)
    rb = jnp.pad(router_b, ((0, 0), (0, pad_e)), constant_values=NEG_INF).reshape(depth, 1, LANES)
    lane = jnp.arange(LANES)
    g128 = jnp.where((lane[:, None] // HEAD_DIM) == (lane[None, :] // HEAD_DIM),
                     1.0 / HEAD_DIM, 0.0).astype(BF16)
    w_g, w_l = _deint_call(w_gu)
    return dict(
        norm_mix_g=norm_mix_g.reshape(depth, 1, D_MODEL),
        norm_ffn_g=norm_ffn_g.reshape(depth, 1, D_MODEL),
        w_in=w_in.astype(BF16),
        w_out=w_out.astype(BF16),
        lgf=ret_log_gamma_f.astype(F32),
        lgb=ret_log_gamma_b.astype(F32),
        q_g=jnp.tile(q_norm_g, (1, LANES // HEAD_DIM)).reshape(depth, 1, LANES),
        k_g=jnp.tile(k_norm_g, (1, LANES // HEAD_DIM)).reshape(depth, 1, LANES),
        sink=attn_sink.astype(F32) * LOG2E,
        rw_hi=rw_hi, rw_lo=rw_lo, rb=rb,
        w_g=w_g,
        w_l=w_l,
        b_g=b_gu[..., 0::2].reshape(depth, N_EXPERTS, 1, D_FF),
        b_l=b_gu[..., 1::2].reshape(depth, N_EXPERTS, 1, D_FF),
        w_dn=w_dn.astype(BF16),
        b_dn=b_dn.reshape(depth, N_EXPERTS, 1, D_MODEL),
        g128=g128,
        bias=_attn_bias(),
    )


def kernel(x_prompt, x_sample, c_prompt, c_sample, ada_w, ada_b, norm_mix_g, w_in, ret_log_gamma_f, ret_log_gamma_b, q_norm_g, k_norm_g, attn_sink, w_out, norm_ffn_g, router_w, router_b, w_gu, b_gu, w_dn, b_dn):
    prm = _prepare(ada_w, ada_b, norm_mix_g, w_in, ret_log_gamma_f, ret_log_gamma_b, q_norm_g,
                   k_norm_g, attn_sink, w_out, norm_ffn_g, router_w, router_b, w_gu, b_gu,
                   w_dn, b_dn)
    bp, bs = c_prompt.shape[0], c_sample.shape[0]
    n_c = bp + bs
    n_pad = -n_c % SUBLANES
    c_all = jnp.concatenate([c_prompt, c_sample, jnp.zeros((n_pad, D_MODEL), F32)], axis=0)
    mods = _ada_call(c_all, ada_w, ada_b)
    depth = ada_w.shape[0]
    mods_p = mods[:, :bp].reshape(depth, bp, 1, 6 * D_MODEL)
    mods_s = mods[:, bp:n_c].reshape(depth, bs, 1, 6 * D_MODEL)
    y_prompt = _trunk(x_prompt, mods_p, prm)
    y_sample = _trunk(x_sample, mods_s, prm)
    return (y_prompt, y_sample)
```

```python
import functools

import jax
import jax.numpy as jnp
from jax import lax
from jax.experimental import pallas as pl
from jax.experimental.pallas import tpu as pltpu

F32 = jnp.float32
BF16 = jnp.bfloat16
I32 = jnp.int32

D_MODEL = 1024
RET_HEADS = 8
HEAD_DIM = 64
RET_WIDTH = RET_HEADS * HEAD_DIM
RET_CHUNK = 128
ATT_HEADS = 8
ATT_KV_HEADS = 2
ATT_WIDTH = ATT_HEADS * HEAD_DIM
ATT_KV_WIDTH = ATT_KV_HEADS * HEAD_DIM
WINDOW = 128
ATT_BLOCK = 128
IN_WIDTH = 4 * RET_WIDTH + ATT_WIDTH + 2 * ATT_KV_WIDTH
N_EXPERTS = 32
TOP_K = 4
D_FF = D_MODEL
SWIGLU_LIMIT = 7.0
SWIGLU_ALPHA = 1.702
RMS_EPS = 1e-6
NEG_INF = -1e30

LANES = 128
SUBLANES = 8
ROW_SLABS = D_MODEL // LANES
VMEM_LIMIT = 56 * 1024 * 1024


def _cparams(sem):
    return pltpu.CompilerParams(dimension_semantics=sem, vmem_limit_bytes=VMEM_LIMIT)


def _sigmoid(x):
    return 1.0 / (1.0 + jnp.exp(-x))


def _split_bf16(x):
    hi = x.astype(BF16)
    lo = (x - hi.astype(F32)).astype(BF16)
    return hi, lo


def _dot(a, b):
    return jnp.dot(a, b, preferred_element_type=F32)


def _dot_nt(a, b):
    return lax.dot_general(a, b, (((1,), (1,)), ((), ())), preferred_element_type=F32)


def _group_mean(x2, g_ref):
    return _dot(x2.astype(BF16), g_ref[...])


def _ada_kernel(c_ref, w_ref, b_ref, o_ref):
    c = c_ref[...]
    a = c * _sigmoid(c)
    a_hi, a_lo = _split_bf16(a)
    w_hi, w_lo = _split_bf16(w_ref[0])
    acc = _dot(a_hi, w_hi) + _dot(a_lo, w_hi) + _dot(a_hi, w_lo)
    o_ref[0] = acc + b_ref[0]


def _ada_call(c, ada_w, ada_b):
    depth = ada_w.shape[0]
    bp = c.shape[0]
    tn = 1536
    n_out = ada_w.shape[2]
    return pl.pallas_call(
        _ada_kernel,
        grid=(depth, n_out // tn),
        in_specs=[
            pl.BlockSpec((bp, D_MODEL), lambda l, j: (0, 0)),
            pl.BlockSpec((1, D_MODEL, tn), lambda l, j: (l, 0, j)),
            pl.BlockSpec((1, 1, tn), lambda l, j: (l, 0, j)),
        ],
        out_specs=pl.BlockSpec((1, bp, tn), lambda l, j: (l, 0, j)),
        out_shape=jax.ShapeDtypeStruct((depth, bp, n_out), F32),
        compiler_params=_cparams(("arbitrary", "arbitrary")),
        name="ada_mod",
    )(c, ada_w, ada_b.reshape(depth, 1, n_out))


DEINT_COLS = 256


def _deint_kernel(w_ref, p_ref, g_ref, l_ref):
    perm = p_ref[...]
    half = DEINT_COLS // 2
    for j in range(w_ref.shape[2] // DEINT_COLS):
        w = w_ref[0, :, j * DEINT_COLS:(j + 1) * DEINT_COLS].astype(BF16)
        r = _dot(w, perm)
        g_ref[0, :, j * half:(j + 1) * half] = r[:, :half].astype(BF16)
        l_ref[0, :, j * half:(j + 1) * half] = r[:, half:].astype(BF16)


def _deint_call(w_gu):
    depth, n_e, d_in, d_out2 = w_gu.shape
    w = w_gu.reshape(depth * n_e, d_in, d_out2)
    tk = 512
    src = jnp.arange(DEINT_COLS)
    dst = jnp.where(src % 2 == 0, src // 2, DEINT_COLS // 2 + src // 2)
    perm = (dst[:, None] == jnp.arange(DEINT_COLS)[None, :]).astype(BF16)
    out = jax.ShapeDtypeStruct((depth * n_e, d_in, d_out2 // 2), BF16)
    blk = lambda e, i: (e, i, 0)
    w_g, w_l = pl.pallas_call(
        _deint_kernel,
        grid=(depth * n_e, d_in // tk),
        in_specs=[pl.BlockSpec((1, tk, d_out2), blk),
                  pl.BlockSpec((DEINT_COLS, DEINT_COLS), lambda e, i: (0, 0))],
        out_specs=[pl.BlockSpec((1, tk, d_out2 // 2), blk)] * 2,
        out_shape=[out, out],
        compiler_params=_cparams(("parallel", "parallel")),
        name="expert_weight_split",
    )(w, perm)
    shape = (depth, n_e, d_in, d_out2 // 2)
    return w_g.reshape(shape), w_l.reshape(shape)


def _modulate(x, gain, shift, scale):
    ms = jnp.mean(x * x, axis=-1, keepdims=True)
    xn = x * lax.rsqrt(ms + RMS_EPS)
    return xn * gain * (1.0 + scale) + shift


def _in_kernel(x_ref, mod_ref, gain_ref, w_ref, g_ref, qg_ref, kg_ref,
               qr_ref, kr_ref, vr_ref, sg_ref, qa_ref, ka_ref, va_ref):
    mod = mod_ref[0]
    h = _modulate(x_ref[...], gain_ref[...], mod[:, 0:D_MODEL], mod[:, D_MODEL:2 * D_MODEL])
    proj = _dot(h.astype(BF16), w_ref[0])
    w = RET_WIDTH
    qr_ref[...] = proj[:, 0:w].astype(BF16)
    kr_ref[...] = (proj[:, w:2 * w] * (HEAD_DIM ** -0.5)).astype(BF16)
    vr_ref[...] = proj[:, 2 * w:3 * w].astype(BF16)
    gr = proj[:, 3 * w:4 * w]
    sg_ref[...] = (gr * _sigmoid(gr)).astype(BF16)
    base = 4 * w
    qg = qg_ref[...] * (HEAD_DIM ** -0.5 * LOG2E)
    wide = g_ref.shape[0]
    for s in range(ATT_WIDTH // wide):
        q = proj[:, base + s * wide: base + (s + 1) * wide]
        msq = _group_mean(q * q, g_ref)
        qa_ref[:, s * wide:(s + 1) * wide] = (q * lax.rsqrt(msq + RMS_EPS) * qg).astype(BF16)
    kb = base + ATT_WIDTH
    k = proj[:, kb:kb + LANES]
    msk = _group_mean(k * k, g_ref.at[0:LANES, 0:LANES])
    kn = k * lax.rsqrt(msk + RMS_EPS) * kg_ref[...]
    v = proj[:, kb + LANES:kb + 2 * LANES]
    lane = lax.broadcasted_iota(I32, kn.shape, 1)
    first = lane < HEAD_DIM
    for src, dst in ((kn, ka_ref), (v, va_ref)):
        rot = pltpu.roll(src, HEAD_DIM, 1)
        dst[:, 0:LANES] = jnp.where(first, src, rot).astype(BF16)
        dst[:, LANES:2 * LANES] = jnp.where(first, rot, src).astype(BF16)


def _in_call(x2, mod, gain, w_in, layer, g256, qg, kg, seq, tm):
    t = x2.shape[0]
    row = lambda i: (i, 0)
    const = lambda i: (0, 0)
    outs = [
        jax.ShapeDtypeStruct((t, RET_WIDTH), BF16),
        jax.ShapeDtypeStruct((t, RET_WIDTH), BF16),
        jax.ShapeDtypeStruct((t, RET_WIDTH), BF16),
        jax.ShapeDtypeStruct((t, RET_WIDTH), BF16),
        jax.ShapeDtypeStruct((t, ATT_WIDTH), BF16),
        jax.ShapeDtypeStruct((t, 2 * LANES), BF16),
        jax.ShapeDtypeStruct((t, 2 * LANES), BF16),
    ]
    return pl.pallas_call(
        _in_kernel,
        grid=(t // tm,),
        in_specs=[
            pl.BlockSpec((tm, D_MODEL), row),
            pl.BlockSpec((1, 1, 6 * D_MODEL), lambda i: ((i * tm) // seq, 0, 0)),
            pl.BlockSpec((1, D_MODEL), const),
            pl.BlockSpec((1, D_MODEL, IN_WIDTH), lambda i: (layer, 0, 0)),
            pl.BlockSpec(g256.shape, const),
            pl.BlockSpec((1, g256.shape[0]), const),
            pl.BlockSpec((1, LANES), const),
        ],
        out_specs=[pl.BlockSpec((tm, o.shape[1]), row) for o in outs],
        out_shape=outs,
        compiler_params=_cparams(("parallel",)),
        name="in_proj",
    )(x2, mod, gain, w_in, g256, qg, kg)


def _ret_kernel(lgf_ref, lgb_ref, q_ref, k_ref, v_ref, sg_ref, g_ref, o_ref,
                uf_ref, ub_ref, sf_ref, sb_ref, s_ref, acc_ref):
    p = pl.program_id(1)
    c = RET_CHUNK
    n_chunks = q_ref.shape[0] // c
    lf0, lf1 = lgf_ref[2 * p], lgf_ref[2 * p + 1]
    lb0, lb1 = lgb_ref[2 * p], lgb_ref[2 * p + 1]
    row = lax.broadcasted_iota(I32, (c, LANES), 0)
    lane = lax.broadcasted_iota(I32, (c, LANES), 1)
    pos = row.astype(F32)
    first = lane < HEAD_DIM
    lf_lane = jnp.where(first, lf0, lf1)
    lb_lane = jnp.where(first, lb0, lb1)
    qd_f = jnp.exp(lf_lane * (pos + 1.0))
    kd_f = jnp.exp(lf_lane * (c - 1.0 - pos))
    qd_b = jnp.exp(lb_lane * (c - pos))
    kd_b = jnp.exp(lb_lane * pos)
    cd_f = jnp.exp(lf_lane * float(c))
    cd_b = jnp.exp(lb_lane * float(c))
    same_head = (row < HEAD_DIM) == first
    row2 = lax.broadcasted_iota(I32, (c, 2 * c), 0)
    col2 = lax.broadcasted_iota(I32, (c, 2 * c), 1)
    head1 = col2 >= c
    diff = (row2 - jnp.where(head1, col2 - c, col2)).astype(F32)
    lf_c = jnp.where(head1, lf1, lf0)
    lb_c = jnp.where(head1, lb1, lb0)
    decay = jnp.where(diff >= 0, jnp.exp(lf_c * jnp.maximum(diff, 0.0)),
                      jnp.exp(lb_c * jnp.maximum(-diff, 0.0)))

    unroll = _pick(n_chunks, (4, 2, 1))

    def updates(g, carry):
        for u in range(unroll):
            n = g * unroll + u
            off = pl.multiple_of(n * c, c)
            k = k_ref[pl.ds(off, c), :].astype(F32)
            vt = v_ref[pl.ds(off, c), :].astype(F32).T.astype(BF16)
            uf_ref[n] = jnp.where(same_head, _dot(vt, (k * kd_f).astype(BF16)), 0.0)
            ub_ref[n] = jnp.where(same_head, _dot(vt, (k * kd_b).astype(BF16)), 0.0)
        return carry

    lax.fori_loop(0, n_chunks // unroll, updates, 0)

    def scan_f(n, state):
        sf_ref[n] = state.astype(BF16)
        return state * cd_f + uf_ref[n]

    def scan_b(i, state):
        n = n_chunks - 1 - i
        sb_ref[n] = state.astype(BF16)
        return state * cd_b + ub_ref[n]

    zero_state = jnp.zeros((LANES, LANES), F32)
    lax.fori_loop(0, n_chunks, scan_f, zero_state)
    lax.fori_loop(0, n_chunks, scan_b, zero_state)

    group = s_ref.shape[0]
    zero = jnp.zeros((c, LANES), BF16)

    def split_heads(x):
        return jnp.concatenate([jnp.where(first, x, zero), jnp.where(first, zero, x)], axis=0)

    def outputs(g, carry):
        offs = [pl.multiple_of((g * group + u) * c, c) for u in range(group)]
        for u in range(group):
            s_ref[u] = _dot_nt(q_ref[pl.ds(offs[u], c), :],
                               split_heads(k_ref[pl.ds(offs[u], c), :]))
        for u in range(group):
            n = g * group + u
            q = q_ref[pl.ds(offs[u], c), :].astype(F32)
            acc_ref[u] = (_dot((s_ref[u] * decay).astype(BF16), split_heads(v_ref[pl.ds(offs[u], c), :]))
                          + _dot_nt((q * qd_f).astype(BF16), sf_ref[n])
                          + _dot_nt((q * qd_b).astype(BF16), sb_ref[n]))
        for u in range(group):
            o = acc_ref[u]
            ms = _group_mean(o * o, g_ref)
            on = o * lax.rsqrt(ms + RMS_EPS)
            o_ref[pl.ds(offs[u], c), :] = (on * sg_ref[pl.ds(offs[u], c), :].astype(F32)).astype(BF16)
        return carry

    lax.fori_loop(0, n_chunks // group, outputs, 0)


def _ret_call(qr, kr, vr, sg, lgf, lgb, g128, batch, seq):
    t = qr.shape[0]
    blk = lambda b, p: (b, p)
    smem = pl.BlockSpec(memory_space=pltpu.SMEM)
    n_chunks = seq // RET_CHUNK
    group = _pick(n_chunks, (8, 4, 2, 1))
    return pl.pallas_call(
        _ret_kernel,
        grid=(batch, RET_HEADS // 2),
        in_specs=[smem, smem] + [pl.BlockSpec((seq, LANES), blk)] * 4
                 + [pl.BlockSpec((LANES, LANES), lambda b, p: (0, 0))],
        out_specs=pl.BlockSpec((seq, LANES), blk),
        out_shape=jax.ShapeDtypeStruct((t, RET_WIDTH), BF16),
        scratch_shapes=[pltpu.VMEM((n_chunks, LANES, LANES), F32),
                        pltpu.VMEM((n_chunks, LANES, LANES), F32),
                        pltpu.VMEM((n_chunks, LANES, LANES), BF16),
                        pltpu.VMEM((n_chunks, LANES, LANES), BF16),
                        pltpu.VMEM((group, RET_CHUNK, 2 * RET_CHUNK), F32),
                        pltpu.VMEM((group, RET_CHUNK, LANES), F32)],
        compiler_params=_cparams(("parallel", "parallel")),
        name="retention",
    )(lgf, lgb, qr, kr, vr, sg, g128)


ATT_SUB = 2


ATT_STRIP = 32


def _attn_kernel(sink_ref, q_ref, kp_ref, kc_ref, kn_ref, vp_ref, vc_ref, vn_ref,
                 bias_a_ref, bias_b_ref, o_ref, kcat_ref, vcat_ref, s_ref, p_ref, inv_ref):
    blk = ATT_BLOCK
    rows = (ATT_SUB + 2) * blk
    first = lax.broadcasted_iota(I32, (rows, LANES), 1) < HEAD_DIM
    out_first = lax.broadcasted_iota(I32, (ATT_STRIP, LANES), 1) < HEAD_DIM
    kk = jnp.concatenate([kp_ref[...], kc_ref[...], kn_ref[...]], axis=0)
    vv = jnp.concatenate([vp_ref[...], vc_ref[...], vn_ref[...]], axis=0)
    zero = jnp.zeros((rows, LANES), BF16)
    pairs_per_kv = ATT_HEADS // ATT_KV_HEADS // 2
    combos = []
    for kv in range(ATT_KV_HEADS):
        kd = kk[:, kv * LANES:(kv + 1) * LANES]
        vd = vv[:, kv * LANES:(kv + 1) * LANES]
        k_lo, k_hi = jnp.where(first, kd, zero), jnp.where(first, zero, kd)
        v_lo, v_hi = jnp.where(first, vd, zero), jnp.where(first, zero, vd)
        for sub in range(ATT_SUB):
            band = slice(sub * blk, (sub + 3) * blk)
            g = kv * ATT_SUB + sub
            kcat_ref[g] = jnp.concatenate([k_lo[band], k_hi[band]], axis=0)
            vcat_ref[g] = jnp.concatenate([v_lo[band], v_hi[band]], axis=0)
            bias_ref = bias_a_ref if sub == 0 else bias_b_ref
            qrows = slice(sub * blk, (sub + 1) * blk)
            for pp in range(pairs_per_kv):
                hp = kv * pairs_per_kv + pp
                combos.append((len(combos), qrows, hp, bias_ref, g))
    for c, qrows, hp, bias_ref, g in combos:
        s_ref[c] = _dot_nt(q_ref[qrows, hp * LANES:(hp + 1) * LANES], kcat_ref[g])
    for c, qrows, hp, bias_ref, g in combos:
        for r0 in range(0, blk, ATT_STRIP):
            strip = slice(r0, r0 + ATT_STRIP)
            inv = []
            for hh in range(2):
                h = 2 * hp + hh
                cols = slice(hh * 3 * blk, (hh + 1) * 3 * blk)
                sh = s_ref[c, strip, cols] + bias_ref[0, h, strip, :]
                sink = sink_ref[h]
                m = jnp.maximum(jnp.max(sh, axis=-1, keepdims=True), sink)
                e = jnp.exp2(sh - m)
                den = jnp.sum(e, axis=-1, keepdims=True) + jnp.exp2(sink - m)
                p_ref[c, strip, cols] = e.astype(BF16)
                inv.append(1.0 / den)
            inv_ref[c, strip, :] = jnp.where(out_first, inv[0], inv[1])
    for c, qrows, hp, bias_ref, g in combos:
        o = _dot(p_ref[c], vcat_ref[g]) * inv_ref[c]
        o_ref[qrows, hp * LANES:(hp + 1) * LANES] = o.astype(BF16)


LOG2E = 1.4426950408889634


def _attn_bias():
    blk = ATT_BLOCK
    i = jnp.arange(blk, dtype=F32)[:, None]
    jj = jnp.arange(3 * blk, dtype=F32)[None, :]
    rel = jnp.abs(i + blk - jj)
    slopes = 2.0 ** (-8.0 * (jnp.arange(ATT_HEADS, dtype=F32) + 1.0) / ATT_HEADS)
    base = jnp.where(rel <= WINDOW, -slopes[:, None, None] * rel * LOG2E, NEG_INF)
    no_prev = jnp.where(jj < blk, NEG_INF, base)
    no_next = jnp.where(jj >= 2 * blk, NEG_INF, base)
    return jnp.stack([no_prev, base, no_next])


def _attn_call(qa, ka, va, sink, bias, batch, seq):
    t = qa.shape[0]
    blk = ATT_BLOCK
    assert seq % (ATT_SUB * blk) == 0
    nb = seq // blk
    ns = nb // ATT_SUB
    cur = lambda b, j: (b * ns + j, 0)
    prev = lambda b, j: (b * nb + jnp.maximum(ATT_SUB * j - 1, 0), 0)
    nxt = lambda b, j: (b * nb + jnp.minimum(ATT_SUB * (j + 1), nb - 1), 0)
    kv_specs = [pl.BlockSpec((blk, 2 * LANES), prev),
                pl.BlockSpec((ATT_SUB * blk, 2 * LANES), cur),
                pl.BlockSpec((blk, 2 * LANES), nxt)]
    bias_block = (1, ATT_HEADS, blk, 3 * blk)
    n_combo = ATT_SUB * ATT_HEADS // 2
    return pl.pallas_call(
        _attn_kernel,
        grid=(batch, ns),
        in_specs=[pl.BlockSpec(memory_space=pltpu.SMEM),
                  pl.BlockSpec((ATT_SUB * blk, ATT_WIDTH), cur)] + kv_specs + kv_specs
                 + [pl.BlockSpec(bias_block, lambda b, j: (jnp.where(j == 0, 0, 1), 0, 0, 0)),
                    pl.BlockSpec(bias_block, lambda b, j: (jnp.where(j == ns - 1, 2, 1), 0, 0, 0))],
        out_specs=pl.BlockSpec((ATT_SUB * blk, ATT_WIDTH), cur),
        out_shape=jax.ShapeDtypeStruct((t, ATT_WIDTH), BF16),
        scratch_shapes=[pltpu.VMEM((ATT_KV_HEADS * ATT_SUB, 6 * blk, LANES), BF16),
                        pltpu.VMEM((ATT_KV_HEADS * ATT_SUB, 6 * blk, LANES), BF16),
                        pltpu.VMEM((n_combo, blk, 6 * blk), F32),
                        pltpu.VMEM((n_combo, blk, 6 * blk), BF16),
                        pltpu.VMEM((n_combo, blk, LANES), F32)],
        compiler_params=_cparams(("parallel", "arbitrary")),
        name="window_attn",
    )(sink, qa, ka, ka, ka, va, va, va, bias, bias)


def _route_kernel(x_ref, or_ref, oa_ref, mod_ref, wo_ref, gain_ref, rw_ref,
                  rb_ref, tri_ref, upper_ref, xo_ref, hf_ref, pos_ref, post_ref, gate_ref, cnt_ref,
                  lg_ref, sel_ref, idx_ref, bef_ref, posf_ref):
    sub = tri_ref.shape[0]
    for h in range(x_ref.shape[0] // sub):
        rows = slice(h * sub, (h + 1) * sub)
        _route_tile(x_ref.at[rows], or_ref.at[rows], oa_ref.at[rows], mod_ref, wo_ref, gain_ref,
                    rw_ref, rb_ref, tri_ref, upper_ref, xo_ref.at[rows], hf_ref.at[rows],
                    pos_ref.at[rows], post_ref.at[h], gate_ref.at[rows], cnt_ref.at[h],
                    lg_ref.at[h], sel_ref.at[h], idx_ref.at[h], bef_ref.at[h], posf_ref.at[h])


ROUTE_STRIP = 64


def _route_tile(x_ref, or_ref, oa_ref, mod_ref, wo_ref, gain_ref, rw_ref,
                rb_ref, tri_ref, upper_ref, xo_ref, hf_ref, pos_ref, post_ref, gate_ref, cnt_ref,
                lg_ref, sel_ref, idx_ref, bef_ref, posf_ref):
    mod = mod_ref[0]
    d = D_MODEL
    sub = lg_ref.shape[0]
    mix = _dot(or_ref[...], wo_ref[0, 0:RET_WIDTH, :]) + _dot(oa_ref[...], wo_ref[0, RET_WIDTH:, :])
    x1 = x_ref[...] + mod[:, 2 * d:3 * d] * mix
    xo_ref[...] = x1
    hf = _modulate(x1, gain_ref[...], mod[:, 3 * d:4 * d], mod[:, 4 * d:5 * d])
    h_hi, h_lo = _split_bf16(hf)
    hf_ref[...] = h_hi
    both = _dot(h_hi, rw_ref[0])
    lg_ref[...] = (both[:, 0:LANES] + both[:, LANES:2 * LANES]
                   + _dot(h_lo, rw_ref[0, :, 0:LANES]) + rb_ref[...])
    lane = lax.broadcasted_iota(I32, (ROUTE_STRIP, LANES), 1)
    lane_f = lane.astype(F32)
    strips =[slice(r0, r0 + ROUTE_STRIP) for r0 in range(0, sub, ROUTE_STRIP)]
    counts = jnp.zeros((1, LANES), F32)
    for strip in strips:
        work = lg_ref[strip, :]
        sel = jnp.zeros(work.shape, F32)
        idx_m = jnp.zeros(work.shape, F32)
        vals = []
        for k in range(TOP_K):
            m = jnp.max(work, axis=-1, keepdims=True)
            idx = jnp.min(jnp.where(work == m, lane_f, float(LANES)), axis=-1, keepdims=True)
            hot = lane_f == idx
            work = jnp.where(hot, -jnp.inf, work)
            sel = sel + jnp.where(hot, 1.0, 0.0)
            idx_m = jnp.where(lane == k, idx, idx_m)
            vals.append(m)
        exps = [jnp.exp(v - vals[0]) for v in vals]
        den = exps[0] + exps[1] + exps[2] + exps[3]
        gates = jnp.zeros(work.shape, F32)
        for k in range(TOP_K):
            gates = jnp.where(lane == k, exps[k] / den, gates)
        gate_ref[strip, :] = gates
        idx_ref[strip, :] = idx_m
        sel_ref[strip, :] = sel.astype(BF16)
        counts = counts + jnp.sum(sel, axis=0, keepdims=True)
    bef_ref[...] = _dot(tri_ref[...], sel_ref[...])
    seg_off = _dot(jnp.broadcast_to(counts, (SUBLANES, LANES)).astype(BF16), upper_ref[...])[0:1]
    for strip in strips:
        slot = bef_ref[strip, :] + seg_off
        idx_m = idx_ref[strip, :]
        pos = jnp.zeros(slot.shape, F32)
        for k in range(TOP_K):
            hot = lane_f == idx_m[:, k:k + 1]
            pk = jnp.sum(jnp.where(hot, slot, 0.0), axis=-1, keepdims=True)
            pos = jnp.where(lane == k, pk, pos)
        posf_ref[strip, :] = pos
        pos_ref[strip, :] = pos.astype(I32)
    post_ref[...] = posf_ref[...].T[0:SUBLANES, :].astype(I32)
    cnt_ref[...] = counts.astype(I32)


def _route_call(x2, o_r, o_a, mod, w_out, layer, gain, rw, rb, tri, upper, seq, sub, tm):
    t = x2.shape[0]
    nt = t // tm
    n_sub = tm // sub
    row = lambda i: (i, 0)
    const = lambda i: (0, 0)
    outs = [
        jax.ShapeDtypeStruct((t, D_MODEL), F32),
        jax.ShapeDtypeStruct((t, D_MODEL), BF16),
        jax.ShapeDtypeStruct((t, LANES), I32),
        jax.ShapeDtypeStruct((t // sub, SUBLANES, sub), I32),
        jax.ShapeDtypeStruct((t, LANES), F32),
        jax.ShapeDtypeStruct((t // sub, 1, LANES), I32),
    ]
    out_specs = [
        pl.BlockSpec((tm, D_MODEL), row),
        pl.BlockSpec((tm, D_MODEL), row),
        pl.BlockSpec((tm, LANES), row),
        pl.BlockSpec((n_sub, SUBLANES, sub), lambda i: (i, 0, 0)),
        pl.BlockSpec((tm, LANES), row),
        pl.BlockSpec((n_sub, 1, LANES), lambda i: (i, 0, 0)),
    ]
    return pl.pallas_call(
        _route_kernel,
        grid=(nt,),
        in_specs=[
            pl.BlockSpec((tm, D_MODEL), row),
            pl.BlockSpec((tm, RET_WIDTH), row),
            pl.BlockSpec((tm, ATT_WIDTH), row),
            pl.BlockSpec((1, 1, 6 * D_MODEL), lambda i: ((i * tm) // seq, 0, 0)),
            pl.BlockSpec((1, D_MODEL, D_MODEL), lambda i: (layer, 0, 0)),
            pl.BlockSpec((1, D_MODEL), const),
            pl.BlockSpec((1, D_MODEL, 2 * LANES), lambda i: (layer, 0, 0)),
            pl.BlockSpec((1, LANES), const),
            pl.BlockSpec((sub, sub), const),
            pl.BlockSpec((LANES, LANES), const),
        ],
        out_specs=out_specs,
        out_shape=outs,
        scratch_shapes=[pltpu.VMEM((n_sub, sub, LANES), F32),
                        pltpu.VMEM((n_sub, sub, LANES), BF16),
                        pltpu.VMEM((n_sub, sub, LANES), F32),
                        pltpu.VMEM((n_sub, sub, LANES), F32),
                        pltpu.VMEM((n_sub, sub, LANES), F32)],
        compiler_params=_cparams(("parallel",)),
        name="out_proj_route",
    )(x2, o_r, o_a, mod, w_out, gain, rw, rb, tri, upper)


def _rows(ref, row, n):
    return ref.at[pl.ds(pl.multiple_of(row * ROW_SLABS, SUBLANES), n * ROW_SLABS), :]


def _segment_copies(cnt_ref, off_ref, dst_ref, tm, make_copy):
    def per_expert(e, carry):
        n = cnt_ref[0, 0, e]
        off = off_ref[0, 0, e]
        dst = dst_ref[0, 0, e]
        done = jnp.int32(0)
        piece = tm
        while piece >= 1:
            @pl.when((n & piece) != 0)
            def _(piece=piece, done=done):
                make_copy(off + done, dst + done, piece).start()
            done = done + (n & piece)
            piece //= 2
        return carry

    lax.fori_loop(0, N_EXPERTS, per_expert, 0)


def _dispatch_kernel(cnt_ref, off_ref, dst_ref, post_ref, hf_ref, xs_in_ref, xs_ref, stage_ref, sems):
    del xs_in_ref
    i = pl.program_id(0)
    tm = hf_ref.shape[0]
    n_slots = TOP_K * tm
    slot = i % 2
    stage = stage_ref.at[slot]
    post = post_ref[0]
    j = lax.broadcasted_iota(I32, (n_slots, tm), 0)
    perm = sum(jnp.where(post[k:k + 1, :] == j, 1.0, 0.0) for k in range(TOP_K))
    srt = _dot(perm.astype(BF16), hf_ref[...])
    for c in range(ROW_SLABS):
        stage[pl.ds(c, n_slots, stride=ROW_SLABS), :] = srt[:, c * LANES:(c + 1) * LANES]
    _segment_copies(cnt_ref, off_ref, dst_ref, tm,
                    lambda o, d, n: pltpu.make_async_copy(_rows(stage, o, n), _rows(xs_ref, d, n),
                                                          sems.at[slot]))

    def wait_tile(s):
        pltpu.make_async_copy(stage_ref.at[s], _rows(xs_ref, 0, n_slots), sems.at[s]).wait()

    @pl.when(i > 0)
    def _():
        wait_tile(1 - slot)

    @pl.when(i == pl.num_programs(0) - 1)
    def _():
        wait_tile(slot)


def _seg_spec(index_map):
    return pl.BlockSpec((1, 1, LANES), index_map, memory_space=pltpu.SMEM)


def _dispatch_call(cnt, seg_off, seg_dst, post, hf, xs_prev, tm):
    t = hf.shape[0]
    tile = lambda i: (i, 0, 0)
    return pl.pallas_call(
        _dispatch_kernel,
        grid=(t // tm,),
        in_specs=[
            _seg_spec(tile), _seg_spec(tile), _seg_spec(tile),
            pl.BlockSpec((1, SUBLANES, tm), tile),
            pl.BlockSpec((tm, D_MODEL), lambda i: (i, 0)),
            pl.BlockSpec(memory_space=pl.ANY),
        ],
        out_specs=pl.BlockSpec(memory_space=pl.ANY),
        out_shape=jax.ShapeDtypeStruct(xs_prev.shape, F32),
        scratch_shapes=[pltpu.VMEM((2, TOP_K * tm * ROW_SLABS, LANES), F32),
                        pltpu.SemaphoreType.DMA((2,))],
        input_output_aliases={5: 0},
        compiler_params=_cparams(("arbitrary",)),
        name="moe_dispatch",
    )(cnt, seg_off, seg_dst, post, hf, xs_prev)


def _moe_kernel(be_ref, nu_ref, xs_ref, wg_ref, wl_ref, bg_ref, bl_ref, wd_ref, bd_ref,
                y_ref, xb_ref):
    del be_ref
    i = pl.program_id(0)
    tm = xb_ref.shape[0]

    @pl.when(i < nu_ref[0])
    def _():
        for c in range(ROW_SLABS):
            xb_ref[:, c * LANES:(c + 1) * LANES] = (
                xs_ref[pl.ds(c, tm, stride=ROW_SLABS), :].astype(BF16))
        x = xb_ref[...]
        g = _dot(x, wg_ref[0, 0]) + bg_ref[0, 0]
        lin = _dot(x, wl_ref[0, 0]) + bl_ref[0, 0]
        glu = jnp.minimum(g, SWIGLU_LIMIT)
        lin = jnp.clip(lin, -SWIGLU_LIMIT, SWIGLU_LIMIT)
        act = (lin + 1.0) * glu * _sigmoid(SWIGLU_ALPHA * glu)
        y = _dot(act.astype(BF16), wd_ref[0, 0]) + bd_ref[0, 0]
        for c in range(ROW_SLABS):
            y_ref[pl.ds(c, tm, stride=ROW_SLABS), :] = y[:, c * LANES:(c + 1) * LANES]

    @pl.when(i >= nu_ref[0])
    def _():
        y_ref[...] = jnp.zeros_like(y_ref)


def _moe_call(block_e, n_used, xs, wg, wl, bg, bl, wd, bd, layer, tm):
    n_tiles = xs.shape[0] // (tm * ROW_SLABS)
    row = lambda i, be, nu: (i, 0)
    wsel = lambda i, be, nu: (layer, be[i], 0, 0)
    grid_spec = pltpu.PrefetchScalarGridSpec(
        num_scalar_prefetch=2,
        grid=(n_tiles,),
        in_specs=[
            pl.BlockSpec((tm * ROW_SLABS, LANES), row),
            pl.BlockSpec((1, 1, D_MODEL, D_FF), wsel),
            pl.BlockSpec((1, 1, D_MODEL, D_FF), wsel),
            pl.BlockSpec((1, 1, 1, D_FF), wsel),
            pl.BlockSpec((1, 1, 1, D_FF), wsel),
            pl.BlockSpec((1, 1, D_FF, D_MODEL), wsel),
            pl.BlockSpec((1, 1, 1, D_MODEL), wsel),
        ],
        out_specs=pl.BlockSpec((tm * ROW_SLABS, LANES), row),
        scratch_shapes=[pltpu.VMEM((tm, D_MODEL), BF16)],
    )
    return pl.pallas_call(
        _moe_kernel,
        grid_spec=grid_spec,
        out_shape=jax.ShapeDtypeStruct(xs.shape, F32),
        compiler_params=_cparams(("arbitrary",)),
        name="moe_experts",
    )(block_e, n_used, xs, wg, wl, bg, bl, wd, bd)


def _combine_kernel(cnt_ref, off_ref, dst_ref, cnt_n_ref, off_n_ref, dst_n_ref,
                    x_ref, pos_ref, gate_ref, mod_ref, y_ref, o_ref, buf_ref, yb_ref, sems):
    i = pl.program_id(0)
    nt = pl.num_programs(0)
    tm = x_ref.shape[0]
    n_slots = TOP_K * tm
    slot = i % 2

    def fetch(tables, s):
        buf = buf_ref.at[s]
        _segment_copies(*tables, tm,
                        lambda o, d, n: pltpu.make_async_copy(_rows(y_ref, d, n), _rows(buf, o, n),
                                                              sems.at[s]))

    @pl.when(i == 0)
    def _():
        fetch((cnt_ref, off_ref, dst_ref), slot)

    @pl.when(i + 1 < nt)
    def _():
        fetch((cnt_n_ref, off_n_ref, dst_n_ref), 1 - slot)

    pltpu.make_async_copy(_rows(y_ref, 0, n_slots), buf_ref.at[slot], sems.at[slot]).wait()
    buf = buf_ref.at[slot]
    for c in range(ROW_SLABS):
        yb_ref[:, c * LANES:(c + 1) * LANES] = buf[pl.ds(c, n_slots, stride=ROW_SLABS), :].astype(BF16)
    pos = pos_ref[...]
    gates = gate_ref[...]
    j = lax.broadcasted_iota(I32, (tm, n_slots), 1)
    wsel = sum(jnp.where(pos[:, k:k + 1] == j, gates[:, k:k + 1], 0.0) for k in range(TOP_K))
    ffn = _dot(wsel.astype(BF16), yb_ref[...])
    o_ref[...] = x_ref[...] + mod_ref[0][:, 5 * D_MODEL:6 * D_MODEL] * ffn


def _combine_call(cnt, seg_off, seg_dst, x2, pos, gates, mod, y, seq, tm):
    t = x2.shape[0]
    nt = t // tm
    row = lambda i: (i, 0)
    tile = lambda i: (i, 0, 0)
    nxt = lambda i: (jnp.minimum(i + 1, nt - 1), 0, 0)
    return pl.pallas_call(
        _combine_kernel,
        grid=(nt,),
        in_specs=[
            _seg_spec(tile), _seg_spec(tile), _seg_spec(tile),
            _seg_spec(nxt), _seg_spec(nxt), _seg_spec(nxt),
            pl.BlockSpec((tm, D_MODEL), row),
            pl.BlockSpec((tm, LANES), row),
            pl.BlockSpec((tm, LANES), row),
            pl.BlockSpec((1, 1, 6 * D_MODEL), lambda i: ((i * tm) // seq, 0, 0)),
            pl.BlockSpec(memory_space=pl.ANY),
        ],
        out_specs=pl.BlockSpec((tm, D_MODEL), row),
        out_shape=jax.ShapeDtypeStruct((t, D_MODEL), F32),
        scratch_shapes=[pltpu.VMEM((2, TOP_K * tm * ROW_SLABS, LANES), F32),
                        pltpu.VMEM((TOP_K * tm, D_MODEL), BF16),
                        pltpu.SemaphoreType.DMA((2,))],
        compiler_params=_cparams(("arbitrary",)),
        name="moe_combine",
    )(cnt, seg_off, seg_dst, cnt, seg_off, seg_dst, x2, pos, gates, mod, y)


def _plan(cnt, tm_moe, n_tiles_moe):
    c = cnt[:, 0, :]
    totals = jnp.sum(c, axis=0)
    padded = (totals + tm_moe - 1) // tm_moe * tm_moe
    pad_end = jnp.cumsum(padded)
    pad_start = pad_end - padded
    seg_dst = pad_start[None, :] + jnp.cumsum(c, axis=0) - c
    seg_off = jnp.cumsum(c, axis=1) - c
    tile_start = jnp.arange(n_tiles_moe, dtype=I32) * tm_moe
    block_e = jnp.minimum(jnp.sum(tile_start[:, None] >= pad_end[None, :N_EXPERTS], axis=1),
                          N_EXPERTS - 1).astype(I32)
    n_used = (pad_end[N_EXPERTS - 1] // tm_moe).astype(I32).reshape(1)
    shape = cnt.shape
    return seg_off.astype(I32).reshape(shape), seg_dst.astype(I32).reshape(shape), block_e, n_used


def _pick(n, pref):
    for c in pref:
        if n % c == 0:
            return c
    raise ValueError(f"no tile size in {pref} divides {n}")


def _trunk(x, mods, prm):
    batch, seq, _ = x.shape
    t = batch * seq
    assert seq % RET_CHUNK == 0 and seq % ATT_BLOCK == 0
    tm_in = _pick(seq, (512, 256, 128))
    tm_route = _pick(seq, (256, 128))
    tm_step = _pick(seq, (2 * tm_route, tm_route))
    tm_moe = 512 if t * TOP_K >= 65536 * 4 else (256 if t * TOP_K >= 16384 else 128)
    n_tiles_moe = (t * TOP_K) // tm_moe + N_EXPERTS
    x2 = x.reshape(t, D_MODEL)
    xs = jnp.zeros((n_tiles_moe * tm_moe * ROW_SLABS, LANES), F32)
    tri = (jnp.arange(tm_route)[:, None] > jnp.arange(tm_route)[None, :]).astype(BF16)
    upper = (jnp.arange(LANES)[:, None] < jnp.arange(LANES)[None, :]).astype(BF16)
    depth = prm["w_in"].shape[0]
    for l in range(depth):
        mod = mods[l]
        qr, kr, vr, sg, qa, ka, va = _in_call(
            x2, mod, prm["norm_mix_g"][l], prm["w_in"], l, prm["g256"],
            prm["q_g"][l], prm["k_g"][l], seq, tm_in)
        o_r = _ret_call(qr, kr, vr, sg, prm["lgf"][l], prm["lgb"][l], prm["g128"], batch, seq)
        o_a = _attn_call(qa, ka, va, prm["sink"][l], prm["bias"], batch, seq)
        x2, hf, pos, post, gates, cnt = _route_call(
            x2, o_r, o_a, mod, prm["w_out"], l, prm["norm_ffn_g"][l],
            prm["rw"], prm["rb"][l], tri, upper, seq, tm_route, tm_step)
        seg_off, seg_dst, block_e, n_used = _plan(cnt, tm_moe, n_tiles_moe)
        xs = _dispatch_call(cnt, seg_off, seg_dst, post, hf, xs, tm_route)
        y = _moe_call(block_e, n_used, xs, prm["w_g"], prm["w_l"], prm["b_g"],
                      prm["b_l"], prm["w_dn"], prm["b_dn"], l, tm_moe)
        x2 = _combine_call(cnt, seg_off, seg_dst, x2, pos, gates, mod, y, seq, tm_route)
    return x2.reshape(batch, seq, D_MODEL)


def _prepare(ada_w, ada_b, norm_mix_g, w_in, ret_log_gamma_f, ret_log_gamma_b, q_norm_g,
             k_norm_g, attn_sink, w_out, norm_ffn_g, router_w, router_b, w_gu, b_gu, w_dn, b_dn):
    depth = w_in.shape[0]
    pad_e = LANES - N_EXPERTS
    rw = jnp.pad(router_w, ((0, 0), (0, 0), (0, pad_e)))
    rw_hi = rw.astype(BF16)
    rw_lo = (rw - rw_hi.astype(F32)).astype(BF16)
    rb = jnp.pad(router_b, ((0, 0), (0, pad_e)), constant_values=NEG_INF).reshape(depth, 1, LANES)
    lane = jnp.arange(2 * LANES)
    g256 = jnp.where((lane[:, None] // HEAD_DIM) == (lane[None, :] // HEAD_DIM),
                     1.0 / HEAD_DIM, 0.0).astype(BF16)
    g128 = g256[:LANES, :LANES]
    w_g, w_l = _deint_call(w_gu)
    return dict(
        norm_mix_g=norm_mix_g.reshape(depth, 1, D_MODEL),
        norm_ffn_g=norm_ffn_g.reshape(depth, 1, D_MODEL),
        w_in=w_in.astype(BF16),
        w_out=w_out.astype(BF16),
        lgf=ret_log_gamma_f.astype(F32),
        lgb=ret_log_gamma_b.astype(F32),
        q_g=jnp.tile(q_norm_g, (1, 2 * LANES // HEAD_DIM)).reshape(depth, 1, 2 * LANES),
        k_g=jnp.tile(k_norm_g, (1, LANES // HEAD_DIM)).reshape(depth, 1, LANES),
        sink=attn_sink.astype(F32) * LOG2E,
        rw=jnp.concatenate([rw_hi, rw_lo], axis=2), rb=rb,
        w_g=w_g,
        w_l=w_l,
        b_g=b_gu[..., 0::2].reshape(depth, N_EXPERTS, 1, D_FF),
        b_l=b_gu[..., 1::2].reshape(depth, N_EXPERTS, 1, D_FF),
        w_dn=w_dn.astype(BF16),
        b_dn=b_dn.reshape(depth, N_EXPERTS, 1, D_MODEL),
        g128=g128,
        g256=g256,
        bias=_attn_bias(),
    )


def kernel(x_prompt, x_sample, c_prompt, c_sample, ada_w, ada_b, norm_mix_g, w_in, ret_log_gamma_f, ret_log_gamma_b, q_norm_g, k_norm_g, attn_sink, w_out, norm_ffn_g, router_w, router_b, w_gu, b_gu, w_dn, b_dn):
    prm = _prepare(ada_w, ada_b, norm_mix_g, w_in, ret_log_gamma_f, ret_log_gamma_b, q_norm_g,
                   k_norm_g, attn_sink, w_out, norm_ffn_g, router_w, router_b, w_gu, b_gu,
                   w_dn, b_dn)
    bp, bs = c_prompt.shape[0], c_sample.shape[0]
    n_c = bp + bs
    n_pad = -n_c % SUBLANES
    c_all = jnp.concatenate([c_prompt, c_sample, jnp.zeros((n_pad, D_MODEL), F32)], axis=0)
    mods = _ada_call(c_all, ada_w, ada_b)
    depth = ada_w.shape[0]
    mods_p = mods[:, :bp].reshape(depth, bp, 1, 6 * D_MODEL)
    mods_s = mods[:, bp:n_c].reshape(depth, bs, 1, 6 * D_MODEL)
    y_prompt = _trunk(x_prompt, mods_p, prm)
    y_sample = _trunk(x_sample, mods_s, prm)
    return (y_prompt, y_sample)
```

```python
import functools

import jax
import jax.numpy as jnp
from jax import lax
from jax.experimental import pallas as pl
from jax.experimental.pallas import tpu as pltpu

F32 = jnp.float32
BF16 = jnp.bfloat16
I32 = jnp.int32

D_MODEL = 1024
RET_HEADS = 8
HEAD_DIM = 64
RET_WIDTH = RET_HEADS * HEAD_DIM
RET_CHUNK = 128
ATT_HEADS = 8
ATT_KV_HEADS = 2
ATT_WIDTH = ATT_HEADS * HEAD_DIM
ATT_KV_WIDTH = ATT_KV_HEADS * HEAD_DIM
WINDOW = 128
ATT_BLOCK = 128
IN_WIDTH = 4 * RET_WIDTH + ATT_WIDTH + 2 * ATT_KV_WIDTH
N_EXPERTS = 32
TOP_K = 4
D_FF = D_MODEL
SWIGLU_LIMIT = 7.0
SWIGLU_ALPHA = 1.702
RMS_EPS = 1e-6
NEG_INF = -1e30

LANES = 128
SUBLANES = 8
ROW_SLABS = D_MODEL // LANES
VMEM_LIMIT = 56 * 1024 * 1024


def _cparams(sem):
    return pltpu.CompilerParams(dimension_semantics=sem, vmem_limit_bytes=VMEM_LIMIT)


def _sigmoid(x):
    return 1.0 / (1.0 + jnp.exp(-x))


def _split_bf16(x):
    hi = x.astype(BF16)
    lo = (x - hi.astype(F32)).astype(BF16)
    return hi, lo


def _dot(a, b):
    return jnp.dot(a, b, preferred_element_type=F32)


def _dot_nt(a, b):
    return lax.dot_general(a, b, (((1,), (1,)), ((), ())), preferred_element_type=F32)


def _group_mean(x2, g_ref):
    return _dot(x2.astype(BF16), g_ref[...])


def _ada_kernel(c_ref, w_ref, b_ref, o_ref):
    c = c_ref[...]
    a = c * _sigmoid(c)
    a_hi, a_lo = _split_bf16(a)
    w_hi, w_lo = _split_bf16(w_ref[0])
    acc = _dot(a_hi, w_hi) + _dot(a_lo, w_hi) + _dot(a_hi, w_lo)
    o_ref[0] = acc + b_ref[0]


def _ada_call(c, ada_w, ada_b):
    depth = ada_w.shape[0]
    bp = c.shape[0]
    tn = 1536
    n_out = ada_w.shape[2]
    return pl.pallas_call(
        _ada_kernel,
        grid=(depth, n_out // tn),
        in_specs=[
            pl.BlockSpec((bp, D_MODEL), lambda l, j: (0, 0)),
            pl.BlockSpec((1, D_MODEL, tn), lambda l, j: (l, 0, j)),
            pl.BlockSpec((1, 1, tn), lambda l, j: (l, 0, j)),
        ],
        out_specs=pl.BlockSpec((1, bp, tn), lambda l, j: (l, 0, j)),
        out_shape=jax.ShapeDtypeStruct((depth, bp, n_out), F32),
        compiler_params=_cparams(("arbitrary", "arbitrary")),
        name="ada_mod",
    )(c, ada_w, ada_b.reshape(depth, 1, n_out))


DEINT_COLS = 256


def _deint_kernel(w_ref, p_ref, g_ref, l_ref):
    perm = p_ref[...]
    half = DEINT_COLS // 2
    for j in range(w_ref.shape[2] // DEINT_COLS):
        w = w_ref[0, :, j * DEINT_COLS:(j + 1) * DEINT_COLS].astype(BF16)
        r = _dot(w, perm)
        g_ref[0, :, j * half:(j + 1) * half] = r[:, :half].astype(BF16)
        l_ref[0, :, j * half:(j + 1) * half] = r[:, half:].astype(BF16)


def _deint_call(w_gu):
    depth, n_e, d_in, d_out2 = w_gu.shape
    w = w_gu.reshape(depth * n_e, d_in, d_out2)
    tk = 512
    src = jnp.arange(DEINT_COLS)
    dst = jnp.where(src % 2 == 0, src // 2, DEINT_COLS // 2 + src // 2)
    perm = (dst[:, None] == jnp.arange(DEINT_COLS)[None, :]).astype(BF16)
    out = jax.ShapeDtypeStruct((depth * n_e, d_in, d_out2 // 2), BF16)
    blk = lambda e, i: (e, i, 0)
    w_g, w_l = pl.pallas_call(
        _deint_kernel,
        grid=(depth * n_e, d_in // tk),
        in_specs=[pl.BlockSpec((1, tk, d_out2), blk),
                  pl.BlockSpec((DEINT_COLS, DEINT_COLS), lambda e, i: (0, 0))],
        out_specs=[pl.BlockSpec((1, tk, d_out2 // 2), blk)] * 2,
        out_shape=[out, out],
        compiler_params=_cparams(("parallel", "parallel")),
        name="expert_weight_split",
    )(w, perm)
    shape = (depth, n_e, d_in, d_out2 // 2)
    return w_g.reshape(shape), w_l.reshape(shape)


def _modulate(x, gain, shift, scale):
    ms = jnp.mean(x * x, axis=-1, keepdims=True)
    xn = x * lax.rsqrt(ms + RMS_EPS)
    return xn * gain * (1.0 + scale) + shift


def _in_kernel(x_ref, mod_ref, gain_ref, w_ref, g_ref, qg_ref, kg_ref,
               qr_ref, kr_ref, vr_ref, sg_ref, qa_ref, ka_ref, va_ref):
    mod = mod_ref[0]
    h = _modulate(x_ref[...], gain_ref[...], mod[:, 0:D_MODEL], mod[:, D_MODEL:2 * D_MODEL])
    proj = _dot(h.astype(BF16), w_ref[0])
    w = RET_WIDTH
    qr_ref[...] = proj[:, 0:w].astype(BF16)
    kr_ref[...] = (proj[:, w:2 * w] * (HEAD_DIM ** -0.5)).astype(BF16)
    vr_ref[...] = proj[:, 2 * w:3 * w].astype(BF16)
    gr = proj[:, 3 * w:4 * w]
    sg_ref[...] = (gr * _sigmoid(gr)).astype(BF16)
    base = 4 * w
    qg = qg_ref[...] * (HEAD_DIM ** -0.5 * LOG2E)
    wide = g_ref.shape[0]
    for s in range(ATT_WIDTH // wide):
        q = proj[:, base + s * wide: base + (s + 1) * wide]
        msq = _group_mean(q * q, g_ref)
        qa_ref[:, s * wide:(s + 1) * wide] = (q * lax.rsqrt(msq + RMS_EPS) * qg).astype(BF16)
    kb = base + ATT_WIDTH
    k = proj[:, kb:kb + LANES]
    msk = _group_mean(k * k, g_ref.at[0:LANES, 0:LANES])
    kn = k * lax.rsqrt(msk + RMS_EPS) * kg_ref[...]
    v = proj[:, kb + LANES:kb + 2 * LANES]
    lane = lax.broadcasted_iota(I32, kn.shape, 1)
    first = lane < HEAD_DIM
    for src, dst in ((kn, ka_ref), (v, va_ref)):
        rot = pltpu.roll(src, HEAD_DIM, 1)
        dst[:, 0:LANES] = jnp.where(first, src, rot).astype(BF16)
        dst[:, LANES:2 * LANES] = jnp.where(first, rot, src).astype(BF16)


def _in_call(x2, mod, gain, w_in, layer, g256, qg, kg, seq, tm):
    t = x2.shape[0]
    row = lambda i: (i, 0)
    const = lambda i: (0, 0)
    outs = [
        jax.ShapeDtypeStruct((t, RET_WIDTH), BF16),
        jax.ShapeDtypeStruct((t, RET_WIDTH), BF16),
        jax.ShapeDtypeStruct((t, RET_WIDTH), BF16),
        jax.ShapeDtypeStruct((t, RET_WIDTH), BF16),
        jax.ShapeDtypeStruct((t, ATT_WIDTH), BF16),
        jax.ShapeDtypeStruct((t, 2 * LANES), BF16),
        jax.ShapeDtypeStruct((t, 2 * LANES), BF16),
    ]
    return pl.pallas_call(
        _in_kernel,
        grid=(t // tm,),
        in_specs=[
            pl.BlockSpec((tm, D_MODEL), row),
            pl.BlockSpec((1, 1, 6 * D_MODEL), lambda i: ((i * tm) // seq, 0, 0)),
            pl.BlockSpec((1, D_MODEL), const),
            pl.BlockSpec((1, D_MODEL, IN_WIDTH), lambda i: (layer, 0, 0)),
            pl.BlockSpec(g256.shape, const),
            pl.BlockSpec((1, g256.shape[0]), const),
            pl.BlockSpec((1, LANES), const),
        ],
        out_specs=[pl.BlockSpec((tm, o.shape[1]), row) for o in outs],
        out_shape=outs,
        compiler_params=_cparams(("parallel",)),
        name="in_proj",
    )(x2, mod, gain, w_in, g256, qg, kg)


def _ret_kernel(lgf_ref, lgb_ref, q_ref, k_ref, v_ref, sg_ref, g_ref, o_ref,
                uf_ref, ub_ref, sf_ref, sb_ref, s_ref, acc_ref):
    p = pl.program_id(1)
    c = RET_CHUNK
    n_chunks = q_ref.shape[0] // c
    lf0, lf1 = lgf_ref[2 * p], lgf_ref[2 * p + 1]
    lb0, lb1 = lgb_ref[2 * p], lgb_ref[2 * p + 1]
    row = lax.broadcasted_iota(I32, (c, LANES), 0)
    lane = lax.broadcasted_iota(I32, (c, LANES), 1)
    pos = row.astype(F32)
    first = lane < HEAD_DIM
    lf_lane = jnp.where(first, lf0, lf1)
    lb_lane = jnp.where(first, lb0, lb1)
    qd_f = jnp.exp(lf_lane * (pos + 1.0))
    kd_f = jnp.exp(lf_lane * (c - 1.0 - pos))
    qd_b = jnp.exp(lb_lane * (c - pos))
    kd_b = jnp.exp(lb_lane * pos)
    cd_f = jnp.exp(lf_lane * float(c))
    cd_b = jnp.exp(lb_lane * float(c))
    same_head = (row < HEAD_DIM) == first
    row2 = lax.broadcasted_iota(I32, (c, 2 * c), 0)
    col2 = lax.broadcasted_iota(I32, (c, 2 * c), 1)
    head1 = col2 >= c
    diff = (row2 - jnp.where(head1, col2 - c, col2)).astype(F32)
    lf_c = jnp.where(head1, lf1, lf0)
    lb_c = jnp.where(head1, lb1, lb0)
    decay = jnp.where(diff >= 0, jnp.exp(lf_c * jnp.maximum(diff, 0.0)),
                      jnp.exp(lb_c * jnp.maximum(-diff, 0.0)))

    unroll = _pick(n_chunks, (8, 4, 2, 1))

    def updates(g, carry):
        for u in range(unroll):
            n = g * unroll + u
            off = pl.multiple_of(n * c, c)
            k = k_ref[pl.ds(off, c), :].astype(F32)
            vt = v_ref[pl.ds(off, c), :].astype(F32).T.astype(BF16)
            uf_ref[n] = jnp.where(same_head, _dot(vt, (k * kd_f).astype(BF16)), 0.0)
            ub_ref[n] = jnp.where(same_head, _dot(vt, (k * kd_b).astype(BF16)), 0.0)
        return carry

    lax.fori_loop(0, n_chunks // unroll, updates, 0)

    def scan_f(n, state):
        sf_ref[n] = state.astype(BF16)
        return state * cd_f + uf_ref[n]

    def scan_b(i, state):
        n = n_chunks - 1 - i
        sb_ref[n] = state.astype(BF16)
        return state * cd_b + ub_ref[n]

    zero_state = jnp.zeros((LANES, LANES), F32)
    lax.fori_loop(0, n_chunks, scan_f, zero_state)
    lax.fori_loop(0, n_chunks, scan_b, zero_state)

    group = s_ref.shape[0]
    zero = jnp.zeros((c, LANES), BF16)

    def split_heads(x):
        return jnp.concatenate([jnp.where(first, x, zero), jnp.where(first, zero, x)], axis=0)

    def outputs(g, carry):
        offs = [pl.multiple_of((g * group + u) * c, c) for u in range(group)]

        def scores(u):
            s_ref[u] = _dot_nt(q_ref[pl.ds(offs[u], c), :],
                               split_heads(k_ref[pl.ds(offs[u], c), :]))

        def mix(u):
            n = g * group + u
            q = q_ref[pl.ds(offs[u], c), :].astype(F32)
            acc_ref[u] = (_dot((s_ref[u] * decay).astype(BF16), split_heads(v_ref[pl.ds(offs[u], c), :]))
                          + _dot_nt((q * qd_f).astype(BF16), sf_ref[n])
                          + _dot_nt((q * qd_b).astype(BF16), sb_ref[n]))

        def finish(u):
            o = acc_ref[u]
            ms = _group_mean(o * o, g_ref)
            on = o * lax.rsqrt(ms + RMS_EPS)
            o_ref[pl.ds(offs[u], c), :] = (on * sg_ref[pl.ds(offs[u], c), :].astype(F32)).astype(BF16)

        for stage in (scores, mix, finish):
            for u in range(group):
                stage(u)
        return carry

    lax.fori_loop(0, n_chunks // group, outputs, 0)


def _ret_call(qr, kr, vr, sg, lgf, lgb, g128, batch, seq):
    t = qr.shape[0]
    blk = lambda b, p: (b, p)
    smem = pl.BlockSpec(memory_space=pltpu.SMEM)
    n_chunks = seq // RET_CHUNK
    group = _pick(n_chunks, (8, 4, 2, 1))
    return pl.pallas_call(
        _ret_kernel,
        grid=(batch, RET_HEADS // 2),
        in_specs=[smem, smem] + [pl.BlockSpec((seq, LANES), blk)] * 4
                 + [pl.BlockSpec((LANES, LANES), lambda b, p: (0, 0))],
        out_specs=pl.BlockSpec((seq, LANES), blk),
        out_shape=jax.ShapeDtypeStruct((t, RET_WIDTH), BF16),
        scratch_shapes=[pltpu.VMEM((n_chunks, LANES, LANES), F32),
                        pltpu.VMEM((n_chunks, LANES, LANES), F32),
                        pltpu.VMEM((n_chunks, LANES, LANES), BF16),
                        pltpu.VMEM((n_chunks, LANES, LANES), BF16),
                        pltpu.VMEM((group, RET_CHUNK, 2 * RET_CHUNK), F32),
                        pltpu.VMEM((group, RET_CHUNK, LANES), F32)],
        compiler_params=_cparams(("parallel", "parallel")),
        name="retention",
    )(lgf, lgb, qr, kr, vr, sg, g128)


ATT_SUB = 2


ATT_STRIP = 32


def _attn_kernel(sink_ref, q_ref, kp_ref, kc_ref, kn_ref, vp_ref, vc_ref, vn_ref,
                 bias_a_ref, bias_b_ref, o_ref, kcat_ref, vcat_ref, s_ref, p_ref, inv_ref):
    blk = ATT_BLOCK
    rows = (ATT_SUB + 2) * blk
    first = lax.broadcasted_iota(I32, (rows, LANES), 1) < HEAD_DIM
    out_first = lax.broadcasted_iota(I32, (ATT_STRIP, LANES), 1) < HEAD_DIM
    kk = jnp.concatenate([kp_ref[...], kc_ref[...], kn_ref[...]], axis=0)
    vv = jnp.concatenate([vp_ref[...], vc_ref[...], vn_ref[...]], axis=0)
    zero = jnp.zeros((rows, LANES), BF16)
    pairs_per_kv = ATT_HEADS // ATT_KV_HEADS // 2
    combos = []
    for kv in range(ATT_KV_HEADS):
        kd = kk[:, kv * LANES:(kv + 1) * LANES]
        vd = vv[:, kv * LANES:(kv + 1) * LANES]
        k_lo, k_hi = jnp.where(first, kd, zero), jnp.where(first, zero, kd)
        v_lo, v_hi = jnp.where(first, vd, zero), jnp.where(first, zero, vd)
        for sub in range(ATT_SUB):
            band = slice(sub * blk, (sub + 3) * blk)
            g = kv * ATT_SUB + sub
            kcat_ref[g] = jnp.concatenate([k_lo[band], k_hi[band]], axis=0)
            vcat_ref[g] = jnp.concatenate([v_lo[band], v_hi[band]], axis=0)
            bias_ref = bias_a_ref if sub == 0 else bias_b_ref
            qrows = slice(sub * blk, (sub + 1) * blk)
            for pp in range(pairs_per_kv):
                hp = kv * pairs_per_kv + pp
                combos.append((len(combos), qrows, hp, bias_ref, g))
    def scores(combo):
        c, qrows, hp, _, g = combo
        s_ref[c] = _dot_nt(q_ref[qrows, hp * LANES:(hp + 1) * LANES], kcat_ref[g])

    def softmax(combo):
        c, _, hp, bias_ref, _ = combo
        for r0 in range(0, blk, ATT_STRIP):
            strip = slice(r0, r0 + ATT_STRIP)
            inv = []
            for hh in range(2):
                h = 2 * hp + hh
                cols = slice(hh * 3 * blk, (hh + 1) * 3 * blk)
                sh = s_ref[c, strip, cols] + bias_ref[0, h, strip, :]
                sink = sink_ref[h]
                m = jnp.maximum(jnp.max(sh, axis=-1, keepdims=True), sink)
                e = jnp.exp2(sh - m)
                den = jnp.sum(e, axis=-1, keepdims=True) + jnp.exp2(sink - m)
                p_ref[c, strip, cols] = e.astype(BF16)
                inv.append(1.0 / den)
            inv_ref[c, strip, :] = jnp.where(out_first, inv[0], inv[1])

    def values(combo):
        c, qrows, hp, _, g = combo
        o = _dot(p_ref[c], vcat_ref[g]) * inv_ref[c]
        o_ref[qrows, hp * LANES:(hp + 1) * LANES] = o.astype(BF16)

    scores(combos[0])
    for n, combo in enumerate(combos):
        if n + 1 < len(combos):
            scores(combos[n + 1])
        softmax(combo)
        if n >= 1:
            values(combos[n - 1])
    values(combos[-1])


LOG2E = 1.4426950408889634


def _attn_bias():
    blk = ATT_BLOCK
    i = jnp.arange(blk, dtype=F32)[:, None]
    jj = jnp.arange(3 * blk, dtype=F32)[None, :]
    rel = jnp.abs(i + blk - jj)
    slopes = 2.0 ** (-8.0 * (jnp.arange(ATT_HEADS, dtype=F32) + 1.0) / ATT_HEADS)
    base = jnp.where(rel <= WINDOW, -slopes[:, None, None] * rel * LOG2E, NEG_INF)
    no_prev = jnp.where(jj < blk, NEG_INF, base)
    no_next = jnp.where(jj >= 2 * blk, NEG_INF, base)
    return jnp.stack([no_prev, base, no_next])


def _attn_call(qa, ka, va, sink, bias, batch, seq):
    t = qa.shape[0]
    blk = ATT_BLOCK
    assert seq % (ATT_SUB * blk) == 0
    nb = seq // blk
    ns = nb // ATT_SUB
    cur = lambda b, j: (b * ns + j, 0)
    prev = lambda b, j: (b * nb + jnp.maximum(ATT_SUB * j - 1, 0), 0)
    nxt = lambda b, j: (b * nb + jnp.minimum(ATT_SUB * (j + 1), nb - 1), 0)
    kv_specs = [pl.BlockSpec((blk, 2 * LANES), prev),
                pl.BlockSpec((ATT_SUB * blk, 2 * LANES), cur),
                pl.BlockSpec((blk, 2 * LANES), nxt)]
    bias_block = (1, ATT_HEADS, blk, 3 * blk)
    n_combo = ATT_SUB * ATT_HEADS // 2
    return pl.pallas_call(
        _attn_kernel,
        grid=(batch, ns),
        in_specs=[pl.BlockSpec(memory_space=pltpu.SMEM),
                  pl.BlockSpec((ATT_SUB * blk, ATT_WIDTH), cur)] + kv_specs + kv_specs
                 + [pl.BlockSpec(bias_block, lambda b, j: (jnp.where(j == 0, 0, 1), 0, 0, 0)),
                    pl.BlockSpec(bias_block, lambda b, j: (jnp.where(j == ns - 1, 2, 1), 0, 0, 0))],
        out_specs=pl.BlockSpec((ATT_SUB * blk, ATT_WIDTH), cur),
        out_shape=jax.ShapeDtypeStruct((t, ATT_WIDTH), BF16),
        scratch_shapes=[pltpu.VMEM((ATT_KV_HEADS * ATT_SUB, 6 * blk, LANES), BF16),
                        pltpu.VMEM((ATT_KV_HEADS * ATT_SUB, 6 * blk, LANES), BF16),
                        pltpu.VMEM((n_combo, blk, 6 * blk), F32),
                        pltpu.VMEM((n_combo, blk, 6 * blk), BF16),
                        pltpu.VMEM((n_combo, blk, LANES), F32)],
        compiler_params=_cparams(("parallel", "arbitrary")),
        name="window_attn",
    )(sink, qa, ka, ka, ka, va, va, va, bias, bias)


def _route_kernel(x_ref, or_ref, oa_ref, mod_ref, wo_ref, gain_ref, rw_ref,
                  rb_ref, tri_ref, upper_ref, xo_ref, hf_ref, pos_ref, post_ref, gate_ref, cnt_ref,
                  lg_ref, sel_ref, idx_ref, bef_ref, posf_ref):
    sub = tri_ref.shape[0]
    tiles = range(x_ref.shape[0] // sub)
    rows = [slice(h * sub, (h + 1) * sub) for h in tiles]
    for h in tiles:
        _route_logits(x_ref.at[rows[h]], or_ref.at[rows[h]], oa_ref.at[rows[h]], mod_ref, wo_ref,
                      gain_ref, rw_ref, rb_ref, xo_ref.at[rows[h]], hf_ref.at[rows[h]], lg_ref.at[h])
    counts = [_route_topk(lg_ref.at[h], gate_ref.at[rows[h]], idx_ref.at[h], sel_ref.at[h])
              for h in tiles]
    for h in tiles:
        _route_slots(counts[h], tri_ref, upper_ref, sel_ref.at[h], idx_ref.at[h], bef_ref.at[h],
                     posf_ref.at[h], pos_ref.at[rows[h]], post_ref.at[h], cnt_ref.at[h])


ROUTE_STRIP = 64


def _route_logits(x_ref, or_ref, oa_ref, mod_ref, wo_ref, gain_ref, rw_ref, rb_ref,
                  xo_ref, hf_ref, lg_ref):
    mod = mod_ref[0]
    d = D_MODEL
    mix = _dot(or_ref[...], wo_ref[0, 0:RET_WIDTH, :]) + _dot(oa_ref[...], wo_ref[0, RET_WIDTH:, :])
    x1 = x_ref[...] + mod[:, 2 * d:3 * d] * mix
    xo_ref[...] = x1
    hf = _modulate(x1, gain_ref[...], mod[:, 3 * d:4 * d], mod[:, 4 * d:5 * d])
    h_hi, h_lo = _split_bf16(hf)
    hf_ref[...] = h_hi
    both = _dot(h_hi, rw_ref[0])
    lg_ref[...] = (both[:, 0:LANES] + both[:, LANES:2 * LANES]
                   + _dot(h_lo, rw_ref[0, :, 0:LANES]) + rb_ref[...])


def _route_strips(n_rows):
    lane = lax.broadcasted_iota(I32, (ROUTE_STRIP, LANES), 1)
    return lane, lane.astype(F32), [slice(r0, r0 + ROUTE_STRIP) for r0 in range(0, n_rows, ROUTE_STRIP)]


def _route_topk(lg_ref, gate_ref, idx_ref, sel_ref):
    lane, lane_f, strips = _route_strips(lg_ref.shape[0])
    counts = jnp.zeros((1, LANES), F32)
    for strip in strips:
        work = lg_ref[strip, :]
        sel = jnp.zeros(work.shape, F32)
        idx_m = jnp.zeros(work.shape, F32)
        vals = []
        for k in range(TOP_K):
            m = jnp.max(work, axis=-1, keepdims=True)
            idx = jnp.min(jnp.where(work == m, lane_f, float(LANES)), axis=-1, keepdims=True)
            hot = lane_f == idx
            work = jnp.where(hot, -jnp.inf, work)
            sel = sel + jnp.where(hot, 1.0, 0.0)
            idx_m = jnp.where(lane == k, idx, idx_m)
            vals.append(m)
        exps = [jnp.exp(v - vals[0]) for v in vals]
        den = exps[0] + exps[1] + exps[2] + exps[3]
        gates = jnp.zeros(work.shape, F32)
        for k in range(TOP_K):
            gates = jnp.where(lane == k, exps[k] / den, gates)
        gate_ref[strip, :] = gates
        idx_ref[strip, :] = idx_m
        sel_ref[strip, :] = sel.astype(BF16)
        counts = counts + jnp.sum(sel, axis=0, keepdims=True)
    return counts


def _route_slots(counts, tri_ref, upper_ref, sel_ref, idx_ref, bef_ref, posf_ref, pos_ref, post_ref,
                 cnt_ref):
    lane, lane_f, strips = _route_strips(sel_ref.shape[0])
    bef_ref[...] = _dot(tri_ref[...], sel_ref[...])
    seg_off = _dot(jnp.broadcast_to(counts, (SUBLANES, LANES)).astype(BF16), upper_ref[...])[0:1]
    for strip in strips:
        slot = bef_ref[strip, :] + seg_off
        idx_m = idx_ref[strip, :]
        pos = jnp.zeros(slot.shape, F32)
        for k in range(TOP_K):
            hot = lane_f == idx_m[:, k:k + 1]
            pk = jnp.sum(jnp.where(hot, slot, 0.0), axis=-1, keepdims=True)
            pos = jnp.where(lane == k, pk, pos)
        posf_ref[strip, :] = pos
        pos_ref[strip, :] = pos.astype(I32)
    post_ref[...] = posf_ref[...].T[0:SUBLANES, :].astype(I32)
    cnt_ref[...] = counts.astype(I32)


def _route_call(x2, o_r, o_a, mod, w_out, layer, gain, rw, rb, tri, upper, seq, sub, tm):
    t = x2.shape[0]
    nt = t // tm
    n_sub = tm // sub
    row = lambda i: (i, 0)
    const = lambda i: (0, 0)
    outs = [
        jax.ShapeDtypeStruct((t, D_MODEL), F32),
        jax.ShapeDtypeStruct((t, D_MODEL), BF16),
        jax.ShapeDtypeStruct((t, LANES), I32),
        jax.ShapeDtypeStruct((t // sub, SUBLANES, sub), I32),
        jax.ShapeDtypeStruct((t, LANES), F32),
        jax.ShapeDtypeStruct((t // sub, 1, LANES), I32),
    ]
    out_specs = [
        pl.BlockSpec((tm, D_MODEL), row),
        pl.BlockSpec((tm, D_MODEL), row),
        pl.BlockSpec((tm, LANES), row),
        pl.BlockSpec((n_sub, SUBLANES, sub), lambda i: (i, 0, 0)),
        pl.BlockSpec((tm, LANES), row),
        pl.BlockSpec((n_sub, 1, LANES), lambda i: (i, 0, 0)),
    ]
    return pl.pallas_call(
        _route_kernel,
        grid=(nt,),
        in_specs=[
            pl.BlockSpec((tm, D_MODEL), row),
            pl.BlockSpec((tm, RET_WIDTH), row),
            pl.BlockSpec((tm, ATT_WIDTH), row),
            pl.BlockSpec((1, 1, 6 * D_MODEL), lambda i: ((i * tm) // seq, 0, 0)),
            pl.BlockSpec((1, D_MODEL, D_MODEL), lambda i: (layer, 0, 0)),
            pl.BlockSpec((1, D_MODEL), const),
            pl.BlockSpec((1, D_MODEL, 2 * LANES), lambda i: (layer, 0, 0)),
            pl.BlockSpec((1, LANES), const),
            pl.BlockSpec((sub, sub), const),
            pl.BlockSpec((LANES, LANES), const),
        ],
        out_specs=out_specs,
        out_shape=outs,
        scratch_shapes=[pltpu.VMEM((n_sub, sub, LANES), F32),
                        pltpu.VMEM((n_sub, sub, LANES), BF16),
                        pltpu.VMEM((n_sub, sub, LANES), F32),
                        pltpu.VMEM((n_sub, sub, LANES), F32),
                        pltpu.VMEM((n_sub, sub, LANES), F32)],
        compiler_params=_cparams(("parallel",)),
        name="out_proj_route",
    )(x2, o_r, o_a, mod, w_out, gain, rw, rb, tri, upper)


def _rows(ref, row, n):
    return ref.at[pl.ds(pl.multiple_of(row * ROW_SLABS, SUBLANES), n * ROW_SLABS), :]


def _segment_copies(cnt_ref, off_ref, dst_ref, tm, make_copy):
    unroll = 4

    def expert_group(g, carry):
        for u in range(unroll):
            e = g * unroll + u
            n = cnt_ref[0, 0, e]
            off = off_ref[0, 0, e]
            dst = dst_ref[0, 0, e]
            piece = tm
            while piece >= 1:
                done = n & (-2 * piece)

                @pl.when((n & piece) != 0)
                def _(piece=piece, done=done):
                    make_copy(off + done, dst + done, piece).start()
                piece //= 2
        return carry

    lax.fori_loop(0, N_EXPERTS // unroll, expert_group, 0)


def _dispatch_kernel(cnt_ref, off_ref, dst_ref, post_ref, hf_ref, xs_in_ref, xs_ref, stage_ref, sems):
    del xs_in_ref
    i = pl.program_id(0)
    tm = hf_ref.shape[0]
    n_slots = TOP_K * tm
    slot = i % 2
    stage = stage_ref.at[slot]
    post = post_ref[0]
    j = lax.broadcasted_iota(I32, (n_slots, tm), 0)
    perm = sum(jnp.where(post[k:k + 1, :] == j, 1.0, 0.0) for k in range(TOP_K))
    srt = _dot(perm.astype(BF16), hf_ref[...])
    for c in range(ROW_SLABS):
        stage[pl.ds(c, n_slots, stride=ROW_SLABS), :] = srt[:, c * LANES:(c + 1) * LANES]
    _segment_copies(cnt_ref, off_ref, dst_ref, tm,
                    lambda o, d, n: pltpu.make_async_copy(_rows(stage, o, n), _rows(xs_ref, d, n),
                                                          sems.at[slot]))

    def wait_tile(s):
        pltpu.make_async_copy(stage_ref.at[s], _rows(xs_ref, 0, n_slots), sems.at[s]).wait()

    @pl.when(i > 0)
    def _():
        wait_tile(1 - slot)

    @pl.when(i == pl.num_programs(0) - 1)
    def _():
        wait_tile(slot)


def _seg_spec(index_map):
    return pl.BlockSpec((1, 1, LANES), index_map, memory_space=pltpu.SMEM)


def _dispatch_call(cnt, seg_off, seg_dst, post, hf, xs_prev, tm):
    t = hf.shape[0]
    tile = lambda i: (i, 0, 0)
    return pl.pallas_call(
        _dispatch_kernel,
        grid=(t // tm,),
        in_specs=[
            _seg_spec(tile), _seg_spec(tile), _seg_spec(tile),
            pl.BlockSpec((1, SUBLANES, tm), tile),
            pl.BlockSpec((tm, D_MODEL), lambda i: (i, 0)),
            pl.BlockSpec(memory_space=pl.ANY),
        ],
        out_specs=pl.BlockSpec(memory_space=pl.ANY),
        out_shape=jax.ShapeDtypeStruct(xs_prev.shape, F32),
        scratch_shapes=[pltpu.VMEM((2, TOP_K * tm * ROW_SLABS, LANES), F32),
                        pltpu.SemaphoreType.DMA((2,))],
        input_output_aliases={5: 0},
        compiler_params=_cparams(("arbitrary",)),
        name="moe_dispatch",
    )(cnt, seg_off, seg_dst, post, hf, xs_prev)


def _moe_kernel(be_ref, nu_ref, xs_ref, wg_ref, wl_ref, bg_ref, bl_ref, wd_ref, bd_ref,
                y_ref, xb_ref):
    del be_ref
    i = pl.program_id(0)
    tm = xb_ref.shape[0]

    @pl.when(i < nu_ref[0])
    def _():
        for c in range(ROW_SLABS):
            xb_ref[:, c * LANES:(c + 1) * LANES] = (
                xs_ref[pl.ds(c, tm, stride=ROW_SLABS), :].astype(BF16))
        x = xb_ref[...]
        g = _dot(x, wg_ref[0, 0]) + bg_ref[0, 0]
        lin = _dot(x, wl_ref[0, 0]) + bl_ref[0, 0]
        glu = jnp.minimum(g, SWIGLU_LIMIT)
        lin = jnp.clip(lin, -SWIGLU_LIMIT, SWIGLU_LIMIT)
        act = (lin + 1.0) * glu * _sigmoid(SWIGLU_ALPHA * glu)
        y = _dot(act.astype(BF16), wd_ref[0, 0]) + bd_ref[0, 0]
        for c in range(ROW_SLABS):
            y_ref[pl.ds(c, tm, stride=ROW_SLABS), :] = y[:, c * LANES:(c + 1) * LANES]

    @pl.when(i >= nu_ref[0])
    def _():
        y_ref[...] = jnp.zeros_like(y_ref)


def _moe_call(block_e, n_used, xs, wg, wl, bg, bl, wd, bd, layer, tm):
    n_tiles = xs.shape[0] // (tm * ROW_SLABS)
    row = lambda i, be, nu: (i, 0)
    wsel = lambda i, be, nu: (layer, be[i], 0, 0)
    grid_spec = pltpu.PrefetchScalarGridSpec(
        num_scalar_prefetch=2,
        grid=(n_tiles,),
        in_specs=[
            pl.BlockSpec((tm * ROW_SLABS, LANES), row),
            pl.BlockSpec((1, 1, D_MODEL, D_FF), wsel),
            pl.BlockSpec((1, 1, D_MODEL, D_FF), wsel),
            pl.BlockSpec((1, 1, 1, D_FF), wsel),
            pl.BlockSpec((1, 1, 1, D_FF), wsel),
            pl.BlockSpec((1, 1, D_FF, D_MODEL), wsel),
            pl.BlockSpec((1, 1, 1, D_MODEL), wsel),
        ],
        out_specs=pl.BlockSpec((tm * ROW_SLABS, LANES), row),
        scratch_shapes=[pltpu.VMEM((tm, D_MODEL), BF16)],
    )
    return pl.pallas_call(
        _moe_kernel,
        grid_spec=grid_spec,
        out_shape=jax.ShapeDtypeStruct(xs.shape, F32),
        compiler_params=_cparams(("arbitrary",)),
        name="moe_experts",
    )(block_e, n_used, xs, wg, wl, bg, bl, wd, bd)


def _combine_kernel(cnt_ref, off_ref, dst_ref, cnt_n_ref, off_n_ref, dst_n_ref,
                    x_ref, pos_ref, gate_ref, mod_ref, y_ref, o_ref, buf_ref, yb_ref, sems):
    i = pl.program_id(0)
    nt = pl.num_programs(0)
    tm = x_ref.shape[0]
    n_slots = TOP_K * tm
    slot = i % 2

    def fetch(tables, s):
        buf = buf_ref.at[s]
        _segment_copies(*tables, tm,
                        lambda o, d, n: pltpu.make_async_copy(_rows(y_ref, d, n), _rows(buf, o, n),
                                                              sems.at[s]))

    @pl.when(i == 0)
    def _():
        fetch((cnt_ref, off_ref, dst_ref), slot)

    @pl.when(i + 1 < nt)
    def _():
        fetch((cnt_n_ref, off_n_ref, dst_n_ref), 1 - slot)

    pltpu.make_async_copy(_rows(y_ref, 0, n_slots), buf_ref.at[slot], sems.at[slot]).wait()
    buf = buf_ref.at[slot]
    for c in range(ROW_SLABS):
        yb_ref[:, c * LANES:(c + 1) * LANES] = buf[pl.ds(c, n_slots, stride=ROW_SLABS), :].astype(BF16)
    pos = pos_ref[...]
    gates = gate_ref[...]
    j = lax.broadcasted_iota(I32, (tm, n_slots), 1)
    wsel = sum(jnp.where(pos[:, k:k + 1] == j, gates[:, k:k + 1], 0.0) for k in range(TOP_K))
    ffn = _dot(wsel.astype(BF16), yb_ref[...])
    o_ref[...] = x_ref[...] + mod_ref[0][:, 5 * D_MODEL:6 * D_MODEL] * ffn


def _combine_call(cnt, seg_off, seg_dst, x2, pos, gates, mod, y, seq, tm):
    t = x2.shape[0]
    nt = t // tm
    row = lambda i: (i, 0)
    tile = lambda i: (i, 0, 0)
    nxt = lambda i: (jnp.minimum(i + 1, nt - 1), 0, 0)
    return pl.pallas_call(
        _combine_kernel,
        grid=(nt,),
        in_specs=[
            _seg_spec(tile), _seg_spec(tile), _seg_spec(tile),
            _seg_spec(nxt), _seg_spec(nxt), _seg_spec(nxt),
            pl.BlockSpec((tm, D_MODEL), row),
            pl.BlockSpec((tm, LANES), row),
            pl.BlockSpec((tm, LANES), row),
            pl.BlockSpec((1, 1, 6 * D_MODEL), lambda i: ((i * tm) // seq, 0, 0)),
            pl.BlockSpec(memory_space=pl.ANY),
        ],
        out_specs=pl.BlockSpec((tm, D_MODEL), row),
        out_shape=jax.ShapeDtypeStruct((t, D_MODEL), F32),
        scratch_shapes=[pltpu.VMEM((2, TOP_K * tm * ROW_SLABS, LANES), F32),
                        pltpu.VMEM((TOP_K * tm, D_MODEL), BF16),
                        pltpu.SemaphoreType.DMA((2,))],
        compiler_params=_cparams(("arbitrary",)),
        name="moe_combine",
    )(cnt, seg_off, seg_dst, cnt, seg_off, seg_dst, x2, pos, gates, mod, y)


def _plan(cnt, tm_moe, n_tiles_moe):
    c = cnt[:, 0, :]
    totals = jnp.sum(c, axis=0)
    padded = (totals + tm_moe - 1) // tm_moe * tm_moe
    pad_end = jnp.cumsum(padded)
    pad_start = pad_end - padded
    seg_dst = pad_start[None, :] + jnp.cumsum(c, axis=0) - c
    seg_off = jnp.cumsum(c, axis=1) - c
    tile_start = jnp.arange(n_tiles_moe, dtype=I32) * tm_moe
    block_e = jnp.minimum(jnp.sum(tile_start[:, None] >= pad_end[None, :N_EXPERTS], axis=1),
                          N_EXPERTS - 1).astype(I32)
    n_used = (pad_end[N_EXPERTS - 1] // tm_moe).astype(I32).reshape(1)
    shape = cnt.shape
    return seg_off.astype(I32).reshape(shape), seg_dst.astype(I32).reshape(shape), block_e, n_used


def _pick(n, pref):
    for c in pref:
        if n % c == 0:
            return c
    raise ValueError(f"no tile size in {pref} divides {n}")


def _trunk(x, mods, prm):
    batch, seq, _ = x.shape
    t = batch * seq
    assert seq % RET_CHUNK == 0 and seq % ATT_BLOCK == 0
    tm_in = _pick(seq, (512, 256, 128))
    tm_route = _pick(seq, (256, 128))
    tm_step = _pick(seq, (2 * tm_route, tm_route))
    tm_moe = 512 if t * TOP_K >= 65536 * 4 else (256 if t * TOP_K >= 16384 else 128)
    n_tiles_moe = (t * TOP_K) // tm_moe + N_EXPERTS
    x2 = x.reshape(t, D_MODEL)
    xs = jnp.zeros((n_tiles_moe * tm_moe * ROW_SLABS, LANES), F32)
    tri = (jnp.arange(tm_route)[:, None] > jnp.arange(tm_route)[None, :]).astype(BF16)
    upper = (jnp.arange(LANES)[:, None] < jnp.arange(LANES)[None, :]).astype(BF16)
    depth = prm["w_in"].shape[0]
    for l in range(depth):
        mod = mods[l]
        qr, kr, vr, sg, qa, ka, va = _in_call(
            x2, mod, prm["norm_mix_g"][l], prm["w_in"], l, prm["g256"],
            prm["q_g"][l], prm["k_g"][l], seq, tm_in)
        o_r = _ret_call(qr, kr, vr, sg, prm["lgf"][l], prm["lgb"][l], prm["g128"], batch, seq)
        o_a = _attn_call(qa, ka, va, prm["sink"][l], prm["bias"], batch, seq)
        x2, hf, pos, post, gates, cnt = _route_call(
            x2, o_r, o_a, mod, prm["w_out"], l, prm["norm_ffn_g"][l],
            prm["rw"], prm["rb"][l], tri, upper, seq, tm_route, tm_step)
        seg_off, seg_dst, block_e, n_used = _plan(cnt, tm_moe, n_tiles_moe)
        xs = _dispatch_call(cnt, seg_off, seg_dst, post, hf, xs, tm_route)
        y = _moe_call(block_e, n_used, xs, prm["w_g"], prm["w_l"], prm["b_g"],
                      prm["b_l"], prm["w_dn"], prm["b_dn"], l, tm_moe)
        x2 = _combine_call(cnt, seg_off, seg_dst, x2, pos, gates, mod, y, seq, tm_route)
    return x2.reshape(batch, seq, D_MODEL)


def _prepare(ada_w, ada_b, norm_mix_g, w_in, ret_log_gamma_f, ret_log_gamma_b, q_norm_g,
             k_norm_g, attn_sink, w_out, norm_ffn_g, router_w, router_b, w_gu, b_gu, w_dn, b_dn):
    depth = w_in.shape[0]
    pad_e = LANES - N_EXPERTS
    rw = jnp.pad(router_w, ((0, 0), (0, 0), (0, pad_e)))
    rw_hi = rw.astype(BF16)
    rw_lo = (rw - rw_hi.astype(F32)).astype(BF16)
    rb = jnp.pad(router_b, ((0, 0), (0, pad_e)), constant_values=NEG_INF).reshape(depth, 1, LANES)
    lane = jnp.arange(2 * LANES)
    g256 = jnp.where((lane[:, None] // HEAD_DIM) == (lane[None, :] // HEAD_DIM),
                     1.0 / HEAD_DIM, 0.0).astype(BF16)
    g128 = g256[:LANES, :LANES]
    w_g, w_l = _deint_call(w_gu)
    return dict(
        norm_mix_g=norm_mix_g.reshape(depth, 1, D_MODEL),
        norm_ffn_g=norm_ffn_g.reshape(depth, 1, D_MODEL),
        w_in=w_in.astype(BF16),
        w_out=w_out.astype(BF16),
        lgf=ret_log_gamma_f.astype(F32),
        lgb=ret_log_gamma_b.astype(F32),
        q_g=jnp.tile(q_norm_g, (1, 2 * LANES // HEAD_DIM)).reshape(depth, 1, 2 * LANES),
        k_g=jnp.tile(k_norm_g, (1, LANES // HEAD_DIM)).reshape(depth, 1, LANES),
        sink=attn_sink.astype(F32) * LOG2E,
        rw=jnp.concatenate([rw_hi, rw_lo], axis=2), rb=rb,
        w_g=w_g,
        w_l=w_l,
        b_g=b_gu[..., 0::2].reshape(depth, N_EXPERTS, 1, D_FF),
        b_l=b_gu[..., 1::2].reshape(depth, N_EXPERTS, 1, D_FF),
        w_dn=w_dn.astype(BF16),
        b_dn=b_dn.reshape(depth, N_EXPERTS, 1, D_MODEL),
        g128=g128,
        g256=g256,
        bias=_attn_bias(),
    )


def kernel(x_prompt, x_sample, c_prompt, c_sample, ada_w, ada_b, norm_mix_g, w_in, ret_log_gamma_f, ret_log_gamma_b, q_norm_g, k_norm_g, attn_sink, w_out, norm_ffn_g, router_w, router_b, w_gu, b_gu, w_dn, b_dn):
    prm = _prepare(ada_w, ada_b, norm_mix_g, w_in, ret_log_gamma_f, ret_log_gamma_b, q_norm_g,
                   k_norm_g, attn_sink, w_out, norm_ffn_g, router_w, router_b, w_gu, b_gu,
                   w_dn, b_dn)
    bp, bs = c_prompt.shape[0], c_sample.shape[0]
    n_c = bp + bs
    n_pad = -n_c % SUBLANES
    c_all = jnp.concatenate([c_prompt, c_sample, jnp.zeros((n_pad, D_MODEL), F32)], axis=0)
    mods = _ada_call(c_all, ada_w, ada_b)
    depth = ada_w.shape[0]
    mods_p = mods[:, :bp].reshape(depth, bp, 1, 6 * D_MODEL)
    mods_s = mods[:, bp:n_c].reshape(depth, bs, 1, 6 * D_MODEL)
    y_prompt = _trunk(x_prompt, mods_p, prm)
    y_sample = _trunk(x_sample, mods_s, prm)
    return (y_prompt, y_sample)
```

```python
import functools

import jax
import jax.numpy as jnp
from jax import lax
from jax.experimental import pallas as pl
from jax.experimental.pallas import tpu as pltpu

F32 = jnp.float32
BF16 = jnp.bfloat16
I32 = jnp.int32

D_MODEL = 1024
RET_HEADS = 8
HEAD_DIM = 64
RET_WIDTH = RET_HEADS * HEAD_DIM
RET_CHUNK = 128
ATT_HEADS = 8
ATT_KV_HEADS = 2
ATT_WIDTH = ATT_HEADS * HEAD_DIM
ATT_KV_WIDTH = ATT_KV_HEADS * HEAD_DIM
WINDOW = 128
ATT_BLOCK = 128
IN_WIDTH = 4 * RET_WIDTH + ATT_WIDTH + 2 * ATT_KV_WIDTH
N_EXPERTS = 32
TOP_K = 4
D_FF = D_MODEL
SWIGLU_LIMIT = 7.0
SWIGLU_ALPHA = 1.702
RMS_EPS = 1e-6
NEG_INF = -1e30

LANES = 128
SUBLANES = 8
ROW_SLABS = D_MODEL // LANES
VMEM_LIMIT = 56 * 1024 * 1024


def _cparams(sem):
    return pltpu.CompilerParams(dimension_semantics=sem, vmem_limit_bytes=VMEM_LIMIT)


def _sigmoid(x):
    return 1.0 / (1.0 + jnp.exp(-x))


def _split_bf16(x):
    hi = x.astype(BF16)
    lo = (x - hi.astype(F32)).astype(BF16)
    return hi, lo


def _dot(a, b):
    return jnp.dot(a, b, preferred_element_type=F32)


def _dot_nt(a, b):
    return lax.dot_general(a, b, (((1,), (1,)), ((), ())), preferred_element_type=F32)


def _group_mean(x2, g_ref):
    return _dot(x2.astype(BF16), g_ref[...])


def _ada_kernel(c_ref, w_ref, b_ref, o_ref):
    c = c_ref[...]
    a = c * _sigmoid(c)
    a_hi, a_lo = _split_bf16(a)
    w_hi, w_lo = _split_bf16(w_ref[0])
    acc = _dot(a_hi, w_hi) + _dot(a_lo, w_hi) + _dot(a_hi, w_lo)
    o_ref[0] = acc + b_ref[0]


def _ada_call(c, ada_w, ada_b):
    depth = ada_w.shape[0]
    bp = c.shape[0]
    tn = 1536
    n_out = ada_w.shape[2]
    return pl.pallas_call(
        _ada_kernel,
        grid=(depth, n_out // tn),
        in_specs=[
            pl.BlockSpec((bp, D_MODEL), lambda l, j: (0, 0)),
            pl.BlockSpec((1, D_MODEL, tn), lambda l, j: (l, 0, j)),
            pl.BlockSpec((1, 1, tn), lambda l, j: (l, 0, j)),
        ],
        out_specs=pl.BlockSpec((1, bp, tn), lambda l, j: (l, 0, j)),
        out_shape=jax.ShapeDtypeStruct((depth, bp, n_out), F32),
        compiler_params=_cparams(("arbitrary", "arbitrary")),
        name="ada_mod",
    )(c, ada_w, ada_b.reshape(depth, 1, n_out))


DEINT_COLS = 256


def _deint_kernel(w_ref, p_ref, g_ref, l_ref):
    perm = p_ref[...]
    half = DEINT_COLS // 2
    for j in range(w_ref.shape[2] // DEINT_COLS):
        w = w_ref[0, :, j * DEINT_COLS:(j + 1) * DEINT_COLS].astype(BF16)
        r = _dot(w, perm)
        g_ref[0, :, j * half:(j + 1) * half] = r[:, :half].astype(BF16)
        l_ref[0, :, j * half:(j + 1) * half] = r[:, half:].astype(BF16)


def _deint_call(w_gu):
    depth, n_e, d_in, d_out2 = w_gu.shape
    w = w_gu.reshape(depth * n_e, d_in, d_out2)
    tk = 512
    src = jnp.arange(DEINT_COLS)
    dst = jnp.where(src % 2 == 0, src // 2, DEINT_COLS // 2 + src // 2)
    perm = (dst[:, None] == jnp.arange(DEINT_COLS)[None, :]).astype(BF16)
    out = jax.ShapeDtypeStruct((depth * n_e, d_in, d_out2 // 2), BF16)
    blk = lambda e, i: (e, i, 0)
    w_g, w_l = pl.pallas_call(
        _deint_kernel,
        grid=(depth * n_e, d_in // tk),
        in_specs=[pl.BlockSpec((1, tk, d_out2), blk),
                  pl.BlockSpec((DEINT_COLS, DEINT_COLS), lambda e, i: (0, 0))],
        out_specs=[pl.BlockSpec((1, tk, d_out2 // 2), blk)] * 2,
        out_shape=[out, out],
        compiler_params=_cparams(("parallel", "parallel")),
        name="expert_weight_split",
    )(w, perm)
    shape = (depth, n_e, d_in, d_out2 // 2)
    return w_g.reshape(shape), w_l.reshape(shape)


def _modulate(x, gain, shift, scale):
    ms = jnp.mean(x * x, axis=-1, keepdims=True)
    xn = x * lax.rsqrt(ms + RMS_EPS)
    return xn * gain * (1.0 + scale) + shift


def _in_kernel(x_ref, mod_ref, gain_ref, w_ref, g_ref, qg_ref, kg_ref,
               qr_ref, kr_ref, vr_ref, sg_ref, qa_ref, ka_ref, va_ref):
    mod = mod_ref[0]
    h = _modulate(x_ref[...], gain_ref[...], mod[:, 0:D_MODEL], mod[:, D_MODEL:2 * D_MODEL])
    proj = _dot(h.astype(BF16), w_ref[0])
    w = RET_WIDTH
    qr_ref[...] = proj[:, 0:w].astype(BF16)
    kr_ref[...] = (proj[:, w:2 * w] * (HEAD_DIM ** -0.5)).astype(BF16)
    vr_ref[...] = proj[:, 2 * w:3 * w].astype(BF16)
    gr = proj[:, 3 * w:4 * w]
    sg_ref[...] = (gr * _sigmoid(gr)).astype(BF16)
    base = 4 * w
    qg = qg_ref[...] * (HEAD_DIM ** -0.5 * LOG2E)
    wide = g_ref.shape[0]
    for s in range(ATT_WIDTH // wide):
        q = proj[:, base + s * wide: base + (s + 1) * wide]
        msq = _group_mean(q * q, g_ref)
        qa_ref[:, s * wide:(s + 1) * wide] = (q * lax.rsqrt(msq + RMS_EPS) * qg).astype(BF16)
    kb = base + ATT_WIDTH
    k = proj[:, kb:kb + LANES]
    msk = _group_mean(k * k, g_ref.at[0:LANES, 0:LANES])
    kn = k * lax.rsqrt(msk + RMS_EPS) * kg_ref[...]
    v = proj[:, kb + LANES:kb + 2 * LANES]
    lane = lax.broadcasted_iota(I32, kn.shape, 1)
    first = lane < HEAD_DIM
    for src, dst in ((kn, ka_ref), (v, va_ref)):
        rot = pltpu.roll(src, HEAD_DIM, 1)
        dst[:, 0:LANES] = jnp.where(first, src, rot).astype(BF16)
        dst[:, LANES:2 * LANES] = jnp.where(first, rot, src).astype(BF16)


def _in_call(x2, mod, gain, w_in, layer, g256, qg, kg, seq, tm):
    t = x2.shape[0]
    row = lambda i: (i, 0)
    const = lambda i: (0, 0)
    outs = [
        jax.ShapeDtypeStruct((t, RET_WIDTH), BF16),
        jax.ShapeDtypeStruct((t, RET_WIDTH), BF16),
        jax.ShapeDtypeStruct((t, RET_WIDTH), BF16),
        jax.ShapeDtypeStruct((t, RET_WIDTH), BF16),
        jax.ShapeDtypeStruct((t, ATT_WIDTH), BF16),
        jax.ShapeDtypeStruct((t, 2 * LANES), BF16),
        jax.ShapeDtypeStruct((t, 2 * LANES), BF16),
    ]
    return pl.pallas_call(
        _in_kernel,
        grid=(t // tm,),
        in_specs=[
            pl.BlockSpec((tm, D_MODEL), row),
            pl.BlockSpec((1, 1, 6 * D_MODEL), lambda i: ((i * tm) // seq, 0, 0)),
            pl.BlockSpec((1, D_MODEL), const),
            pl.BlockSpec((1, D_MODEL, IN_WIDTH), lambda i: (layer, 0, 0)),
            pl.BlockSpec(g256.shape, const),
            pl.BlockSpec((1, g256.shape[0]), const),
            pl.BlockSpec((1, LANES), const),
        ],
        out_specs=[pl.BlockSpec((tm, o.shape[1]), row) for o in outs],
        out_shape=outs,
        compiler_params=_cparams(("parallel",)),
        name="in_proj",
    )(x2, mod, gain, w_in, g256, qg, kg)


def _ret_kernel(lgf_ref, lgb_ref, q_ref, k_ref, v_ref, sg_ref, g_ref, o_ref,
                uf_ref, ub_ref, sf_ref, sb_ref, s_ref, acc_ref):
    p = pl.program_id(1)
    c = RET_CHUNK
    n_chunks = q_ref.shape[0] // c
    lf0, lf1 = lgf_ref[2 * p], lgf_ref[2 * p + 1]
    lb0, lb1 = lgb_ref[2 * p], lgb_ref[2 * p + 1]
    row = lax.broadcasted_iota(I32, (c, LANES), 0)
    lane = lax.broadcasted_iota(I32, (c, LANES), 1)
    pos = row.astype(F32)
    first = lane < HEAD_DIM
    lf_lane = jnp.where(first, lf0, lf1)
    lb_lane = jnp.where(first, lb0, lb1)
    qd_f = jnp.exp(lf_lane * (pos + 1.0))
    kd_f = jnp.exp(lf_lane * (c - 1.0 - pos))
    qd_b = jnp.exp(lb_lane * (c - pos))
    kd_b = jnp.exp(lb_lane * pos)
    cd_f = jnp.exp(lf_lane * float(c))
    cd_b = jnp.exp(lb_lane * float(c))
    same_head = (row < HEAD_DIM) == first
    row2 = lax.broadcasted_iota(I32, (c, 2 * c), 0)
    col2 = lax.broadcasted_iota(I32, (c, 2 * c), 1)
    head1 = col2 >= c
    diff = (row2 - jnp.where(head1, col2 - c, col2)).astype(F32)
    lf_c = jnp.where(head1, lf1, lf0)
    lb_c = jnp.where(head1, lb1, lb0)
    decay = jnp.where(diff >= 0, jnp.exp(lf_c * jnp.maximum(diff, 0.0)),
                      jnp.exp(lb_c * jnp.maximum(-diff, 0.0)))

    unroll = _pick(n_chunks, (16, 8, 4, 2, 1))

    def updates(g, carry):
        for u in range(unroll):
            n = g * unroll + u
            off = pl.multiple_of(n * c, c)
            k = k_ref[pl.ds(off, c), :].astype(F32)
            vt = v_ref[pl.ds(off, c), :].astype(F32).T.astype(BF16)
            uf_ref[n] = jnp.where(same_head, _dot(vt, (k * kd_f).astype(BF16)), 0.0)
            ub_ref[n] = jnp.where(same_head, _dot(vt, (k * kd_b).astype(BF16)), 0.0)
        return carry

    lax.fori_loop(0, n_chunks // unroll, updates, 0)

    def scan_f(n, state):
        sf_ref[n] = state.astype(BF16)
        return state * cd_f + uf_ref[n]

    def scan_b(i, state):
        n = n_chunks - 1 - i
        sb_ref[n] = state.astype(BF16)
        return state * cd_b + ub_ref[n]

    zero_state = jnp.zeros((LANES, LANES), F32)
    lax.fori_loop(0, n_chunks, scan_f, zero_state)
    lax.fori_loop(0, n_chunks, scan_b, zero_state)

    group = s_ref.shape[0]
    zero = jnp.zeros((c, LANES), BF16)

    def split_heads(x):
        return jnp.concatenate([jnp.where(first, x, zero), jnp.where(first, zero, x)], axis=0)

    def outputs(g, carry):
        offs = [pl.multiple_of((g * group + u) * c, c) for u in range(group)]

        def scores(u):
            s_ref[u] = _dot_nt(q_ref[pl.ds(offs[u], c), :],
                               split_heads(k_ref[pl.ds(offs[u], c), :]))

        def mix(u):
            n = g * group + u
            q = q_ref[pl.ds(offs[u], c), :].astype(F32)
            acc_ref[u] = (_dot((s_ref[u] * decay).astype(BF16), split_heads(v_ref[pl.ds(offs[u], c), :]))
                          + _dot_nt((q * qd_f).astype(BF16), sf_ref[n])
                          + _dot_nt((q * qd_b).astype(BF16), sb_ref[n]))

        def finish(u):
            o = acc_ref[u]
            ms = _group_mean(o * o, g_ref)
            on = o * lax.rsqrt(ms + RMS_EPS)
            o_ref[pl.ds(offs[u], c), :] = (on * sg_ref[pl.ds(offs[u], c), :].astype(F32)).astype(BF16)

        for stage in (scores, mix, finish):
            for u in range(group):
                stage(u)
        return carry

    lax.fori_loop(0, n_chunks // group, outputs, 0)


def _ret_call(qr, kr, vr, sg, lgf, lgb, g128, batch, seq):
    t = qr.shape[0]
    blk = lambda b, p: (b, p)
    smem = pl.BlockSpec(memory_space=pltpu.SMEM)
    n_chunks = seq // RET_CHUNK
    group = _pick(n_chunks, (16, 8, 4, 2, 1))
    return pl.pallas_call(
        _ret_kernel,
        grid=(batch, RET_HEADS // 2),
        in_specs=[smem, smem] + [pl.BlockSpec((seq, LANES), blk)] * 4
                 + [pl.BlockSpec((LANES, LANES), lambda b, p: (0, 0))],
        out_specs=pl.BlockSpec((seq, LANES), blk),
        out_shape=jax.ShapeDtypeStruct((t, RET_WIDTH), BF16),
        scratch_shapes=[pltpu.VMEM((n_chunks, LANES, LANES), F32),
                        pltpu.VMEM((n_chunks, LANES, LANES), F32),
                        pltpu.VMEM((n_chunks, LANES, LANES), BF16),
                        pltpu.VMEM((n_chunks, LANES, LANES), BF16),
                        pltpu.VMEM((group, RET_CHUNK, 2 * RET_CHUNK), F32),
                        pltpu.VMEM((group, RET_CHUNK, LANES), F32)],
        compiler_params=_cparams(("parallel", "parallel")),
        name="retention",
    )(lgf, lgb, qr, kr, vr, sg, g128)


ATT_SUB = 2


ATT_STRIP = 32


def _attn_kernel(sink_ref, q_ref, kp_ref, kc_ref, kn_ref, vp_ref, vc_ref, vn_ref,
                 bias_a_ref, bias_b_ref, o_ref, kcat_ref, vcat_ref, s_ref, p_ref, inv_ref):
    blk = ATT_BLOCK
    rows = (ATT_SUB + 2) * blk
    first = lax.broadcasted_iota(I32, (rows, LANES), 1) < HEAD_DIM
    out_first = lax.broadcasted_iota(I32, (ATT_STRIP, LANES), 1) < HEAD_DIM
    kk = jnp.concatenate([kp_ref[...], kc_ref[...], kn_ref[...]], axis=0)
    vv = jnp.concatenate([vp_ref[...], vc_ref[...], vn_ref[...]], axis=0)
    zero = jnp.zeros((rows, LANES), BF16)
    pairs_per_kv = ATT_HEADS // ATT_KV_HEADS // 2
    combos = []
    for kv in range(ATT_KV_HEADS):
        kd = kk[:, kv * LANES:(kv + 1) * LANES]
        vd = vv[:, kv * LANES:(kv + 1) * LANES]
        k_lo, k_hi = jnp.where(first, kd, zero), jnp.where(first, zero, kd)
        v_lo, v_hi = jnp.where(first, vd, zero), jnp.where(first, zero, vd)
        for sub in range(ATT_SUB):
            band = slice(sub * blk, (sub + 3) * blk)
            g = kv * ATT_SUB + sub
            kcat_ref[g] = jnp.concatenate([k_lo[band], k_hi[band]], axis=0)
            vcat_ref[g] = jnp.concatenate([v_lo[band], v_hi[band]], axis=0)
            bias_ref = bias_a_ref if sub == 0 else bias_b_ref
            qrows = slice(sub * blk, (sub + 1) * blk)
            for pp in range(pairs_per_kv):
                hp = kv * pairs_per_kv + pp
                combos.append((len(combos), qrows, hp, bias_ref, g))
    def scores(combo):
        c, qrows, hp, _, g = combo
        s_ref[c] = _dot_nt(q_ref[qrows, hp * LANES:(hp + 1) * LANES], kcat_ref[g])

    def softmax(combo):
        c, _, hp, bias_ref, _ = combo
        for r0 in range(0, blk, ATT_STRIP):
            strip = slice(r0, r0 + ATT_STRIP)
            inv = []
            for hh in range(2):
                h = 2 * hp + hh
                cols = slice(hh * 3 * blk, (hh + 1) * 3 * blk)
                sh = s_ref[c, strip, cols] + bias_ref[0, h, strip, :]
                sink = sink_ref[h]
                m = jnp.maximum(jnp.max(sh, axis=-1, keepdims=True), sink)
                e = jnp.exp2(sh - m)
                den = jnp.sum(e, axis=-1, keepdims=True) + jnp.exp2(sink - m)
                p_ref[c, strip, cols] = e.astype(BF16)
                inv.append(1.0 / den)
            inv_ref[c, strip, :] = jnp.where(out_first, inv[0], inv[1])

    def values(combo):
        c, qrows, hp, _, g = combo
        o = _dot(p_ref[c], vcat_ref[g]) * inv_ref[c]
        o_ref[qrows, hp * LANES:(hp + 1) * LANES] = o.astype(BF16)

    scores(combos[0])
    for n, combo in enumerate(combos):
        if n + 1 < len(combos):
            scores(combos[n + 1])
        softmax(combo)
        if n >= 1:
            values(combos[n - 1])
    values(combos[-1])


LOG2E = 1.4426950408889634


def _attn_bias():
    blk = ATT_BLOCK
    i = jnp.arange(blk, dtype=F32)[:, None]
    jj = jnp.arange(3 * blk, dtype=F32)[None, :]
    rel = jnp.abs(i + blk - jj)
    slopes = 2.0 ** (-8.0 * (jnp.arange(ATT_HEADS, dtype=F32) + 1.0) / ATT_HEADS)
    base = jnp.where(rel <= WINDOW, -slopes[:, None, None] * rel * LOG2E, NEG_INF)
    no_prev = jnp.where(jj < blk, NEG_INF, base)
    no_next = jnp.where(jj >= 2 * blk, NEG_INF, base)
    return jnp.stack([no_prev, base, no_next])


def _attn_call(qa, ka, va, sink, bias, batch, seq):
    t = qa.shape[0]
    blk = ATT_BLOCK
    assert seq % (ATT_SUB * blk) == 0
    nb = seq // blk
    ns = nb // ATT_SUB
    cur = lambda b, j: (b * ns + j, 0)
    prev = lambda b, j: (b * nb + jnp.maximum(ATT_SUB * j - 1, 0), 0)
    nxt = lambda b, j: (b * nb + jnp.minimum(ATT_SUB * (j + 1), nb - 1), 0)
    kv_specs = [pl.BlockSpec((blk, 2 * LANES), prev),
                pl.BlockSpec((ATT_SUB * blk, 2 * LANES), cur),
                pl.BlockSpec((blk, 2 * LANES), nxt)]
    bias_block = (1, ATT_HEADS, blk, 3 * blk)
    n_combo = ATT_SUB * ATT_HEADS // 2
    return pl.pallas_call(
        _attn_kernel,
        grid=(batch, ns),
        in_specs=[pl.BlockSpec(memory_space=pltpu.SMEM),
                  pl.BlockSpec((ATT_SUB * blk, ATT_WIDTH), cur)] + kv_specs + kv_specs
                 + [pl.BlockSpec(bias_block, lambda b, j: (jnp.where(j == 0, 0, 1), 0, 0, 0)),
                    pl.BlockSpec(bias_block, lambda b, j: (jnp.where(j == ns - 1, 2, 1), 0, 0, 0))],
        out_specs=pl.BlockSpec((ATT_SUB * blk, ATT_WIDTH), cur),
        out_shape=jax.ShapeDtypeStruct((t, ATT_WIDTH), BF16),
        scratch_shapes=[pltpu.VMEM((ATT_KV_HEADS * ATT_SUB, 6 * blk, LANES), BF16),
                        pltpu.VMEM((ATT_KV_HEADS * ATT_SUB, 6 * blk, LANES), BF16),
                        pltpu.VMEM((n_combo, blk, 6 * blk), F32),
                        pltpu.VMEM((n_combo, blk, 6 * blk), BF16),
                        pltpu.VMEM((n_combo, blk, LANES), F32)],
        compiler_params=_cparams(("parallel", "arbitrary")),
        name="window_attn",
    )(sink, qa, ka, ka, ka, va, va, va, bias, bias)


def _route_kernel(x_ref, or_ref, oa_ref, mod_ref, wo_ref, gain_ref, rw_ref,
                  rb_ref, tri_ref, upper_ref, xo_ref, hf_ref, pos_ref, post_ref, gate_ref, cnt_ref,
                  lg_ref, sel_ref, idx_ref, bef_ref, posf_ref):
    sub = tri_ref.shape[0]
    tiles = range(x_ref.shape[0] // sub)
    rows = [slice(h * sub, (h + 1) * sub) for h in tiles]
    for h in tiles:
        _route_logits(x_ref.at[rows[h]], or_ref.at[rows[h]], oa_ref.at[rows[h]], mod_ref, wo_ref,
                      gain_ref, rw_ref, rb_ref, xo_ref.at[rows[h]], hf_ref.at[rows[h]], lg_ref.at[h])
    counts = [_route_topk(lg_ref.at[h], gate_ref.at[rows[h]], idx_ref.at[h], sel_ref.at[h])
              for h in tiles]
    for h in tiles:
        _route_slots(counts[h], tri_ref, upper_ref, sel_ref.at[h], idx_ref.at[h], bef_ref.at[h],
                     posf_ref.at[h], pos_ref.at[rows[h]], post_ref.at[h], cnt_ref.at[h])


ROUTE_STRIP = 128


def _route_logits(x_ref, or_ref, oa_ref, mod_ref, wo_ref, gain_ref, rw_ref, rb_ref,
                  xo_ref, hf_ref, lg_ref):
    mod = mod_ref[0]
    d = D_MODEL
    mix = _dot(or_ref[...], wo_ref[0, 0:RET_WIDTH, :]) + _dot(oa_ref[...], wo_ref[0, RET_WIDTH:, :])
    x1 = x_ref[...] + mod[:, 2 * d:3 * d] * mix
    xo_ref[...] = x1
    hf = _modulate(x1, gain_ref[...], mod[:, 3 * d:4 * d], mod[:, 4 * d:5 * d])
    h_hi, h_lo = _split_bf16(hf)
    hf_ref[...] = h_hi
    both = _dot(h_hi, rw_ref[0])
    lg_ref[...] = (both[:, 0:LANES] + both[:, LANES:2 * LANES]
                   + _dot(h_lo, rw_ref[0, :, 0:LANES]) + rb_ref[...])


def _route_strips(n_rows):
    lane = lax.broadcasted_iota(I32, (ROUTE_STRIP, LANES), 1)
    return lane, lane.astype(F32), [slice(r0, r0 + ROUTE_STRIP) for r0 in range(0, n_rows, ROUTE_STRIP)]


def _route_topk(lg_ref, gate_ref, idx_ref, sel_ref):
    lane, lane_f, strips = _route_strips(lg_ref.shape[0])
    counts = jnp.zeros((1, LANES), F32)
    for strip in strips:
        work = lg_ref[strip, :]
        sel = jnp.zeros(work.shape, F32)
        idx_m = jnp.zeros(work.shape, F32)
        vals = []
        for k in range(TOP_K):
            m = jnp.max(work, axis=-1, keepdims=True)
            idx = jnp.min(jnp.where(work == m, lane_f, float(LANES)), axis=-1, keepdims=True)
            hot = lane_f == idx
            work = jnp.where(hot, -jnp.inf, work)
            sel = sel + jnp.where(hot, 1.0, 0.0)
            idx_m = jnp.where(lane == k, idx, idx_m)
            vals.append(m)
        exps = [jnp.exp(v - vals[0]) for v in vals]
        den = exps[0] + exps[1] + exps[2] + exps[3]
        gates = jnp.zeros(work.shape, F32)
        for k in range(TOP_K):
            gates = jnp.where(lane == k, exps[k] / den, gates)
        gate_ref[strip, :] = gates
        idx_ref[strip, :] = idx_m
        sel_ref[strip, :] = sel.astype(BF16)
        counts = counts + jnp.sum(sel, axis=0, keepdims=True)
    return counts


def _route_slots(counts, tri_ref, upper_ref, sel_ref, idx_ref, bef_ref, posf_ref, pos_ref, post_ref,
                 cnt_ref):
    lane, lane_f, strips = _route_strips(sel_ref.shape[0])
    bef_ref[...] = _dot(tri_ref[...], sel_ref[...])
    seg_off = _dot(jnp.broadcast_to(counts, (SUBLANES, LANES)).astype(BF16), upper_ref[...])[0:1]
    for strip in strips:
        slot = bef_ref[strip, :] + seg_off
        idx_m = idx_ref[strip, :]
        pos = jnp.zeros(slot.shape, F32)
        for k in range(TOP_K):
            hot = lane_f == idx_m[:, k:k + 1]
            pk = jnp.sum(jnp.where(hot, slot, 0.0), axis=-1, keepdims=True)
            pos = jnp.where(lane == k, pk, pos)
        posf_ref[strip, :] = pos
        pos_ref[strip, :] = pos.astype(I32)
    post_ref[...] = posf_ref[...].T[0:SUBLANES, :].astype(I32)
    cnt_ref[...] = counts.astype(I32)


def _route_call(x2, o_r, o_a, mod, w_out, layer, gain, rw, rb, tri, upper, seq, sub, tm):
    t = x2.shape[0]
    nt = t // tm
    n_sub = tm // sub
    row = lambda i: (i, 0)
    const = lambda i: (0, 0)
    outs = [
        jax.ShapeDtypeStruct((t, D_MODEL), F32),
        jax.ShapeDtypeStruct((t, D_MODEL), BF16),
        jax.ShapeDtypeStruct((t, LANES), I32),
        jax.ShapeDtypeStruct((t // sub, SUBLANES, sub), I32),
        jax.ShapeDtypeStruct((t, LANES), F32),
        jax.ShapeDtypeStruct((t // sub, 1, LANES), I32),
    ]
    out_specs = [
        pl.BlockSpec((tm, D_MODEL), row),
        pl.BlockSpec((tm, D_MODEL), row),
        pl.BlockSpec((tm, LANES), row),
        pl.BlockSpec((n_sub, SUBLANES, sub), lambda i: (i, 0, 0)),
        pl.BlockSpec((tm, LANES), row),
        pl.BlockSpec((n_sub, 1, LANES), lambda i: (i, 0, 0)),
    ]
    return pl.pallas_call(
        _route_kernel,
        grid=(nt,),
        in_specs=[
            pl.BlockSpec((tm, D_MODEL), row),
            pl.BlockSpec((tm, RET_WIDTH), row),
            pl.BlockSpec((tm, ATT_WIDTH), row),
            pl.BlockSpec((1, 1, 6 * D_MODEL), lambda i: ((i * tm) // seq, 0, 0)),
            pl.BlockSpec((1, D_MODEL, D_MODEL), lambda i: (layer, 0, 0)),
            pl.BlockSpec((1, D_MODEL), const),
            pl.BlockSpec((1, D_MODEL, 2 * LANES), lambda i: (layer, 0, 0)),
            pl.BlockSpec((1, LANES), const),
            pl.BlockSpec((sub, sub), const),
            pl.BlockSpec((LANES, LANES), const),
        ],
        out_specs=out_specs,
        out_shape=outs,
        scratch_shapes=[pltpu.VMEM((n_sub, sub, LANES), F32),
                        pltpu.VMEM((n_sub, sub, LANES), BF16),
                        pltpu.VMEM((n_sub, sub, LANES), F32),
                        pltpu.VMEM((n_sub, sub, LANES), F32),
                        pltpu.VMEM((n_sub, sub, LANES), F32)],
        compiler_params=_cparams(("parallel",)),
        name="out_proj_route",
    )(x2, o_r, o_a, mod, w_out, gain, rw, rb, tri, upper)


def _rows(ref, row, n):
    return ref.at[pl.ds(pl.multiple_of(row * ROW_SLABS, SUBLANES), n * ROW_SLABS), :]


def _segment_copies(cnt_ref, off_ref, dst_ref, tm, make_copy):
    unroll = 4

    def expert_group(g, carry):
        for u in range(unroll):
            e = g * unroll + u
            n = cnt_ref[0, 0, e]
            off = off_ref[0, 0, e]
            dst = dst_ref[0, 0, e]
            piece = tm
            while piece >= 1:
                done = n & (-2 * piece)

                @pl.when((n & piece) != 0)
                def _(piece=piece, done=done):
                    make_copy(off + done, dst + done, piece).start()
                piece //= 2
        return carry

    lax.fori_loop(0, N_EXPERTS // unroll, expert_group, 0)


def _dispatch_kernel(cnt_ref, off_ref, dst_ref, post_ref, hf_ref, xs_in_ref, xs_ref, stage_ref, sems):
    del xs_in_ref
    i = pl.program_id(0)
    tm = hf_ref.shape[0]
    n_slots = TOP_K * tm
    slot = i % 2
    stage = stage_ref.at[slot]
    post = post_ref[0]
    j = lax.broadcasted_iota(I32, (n_slots, tm), 0)
    perm = sum(jnp.where(post[k:k + 1, :] == j, 1.0, 0.0) for k in range(TOP_K))
    srt = _dot(perm.astype(BF16), hf_ref[...])
    for c in range(ROW_SLABS):
        stage[pl.ds(c, n_slots, stride=ROW_SLABS), :] = srt[:, c * LANES:(c + 1) * LANES]
    _segment_copies(cnt_ref, off_ref, dst_ref, tm,
                    lambda o, d, n: pltpu.make_async_copy(_rows(stage, o, n), _rows(xs_ref, d, n),
                                                          sems.at[slot]))

    def wait_tile(s):
        pltpu.make_async_copy(stage_ref.at[s], _rows(xs_ref, 0, n_slots), sems.at[s]).wait()

    @pl.when(i > 0)
    def _():
        wait_tile(1 - slot)

    @pl.when(i == pl.num_programs(0) - 1)
    def _():
        wait_tile(slot)


def _seg_spec(index_map):
    return pl.BlockSpec((1, 1, LANES), index_map, memory_space=pltpu.SMEM)


def _dispatch_call(cnt, seg_off, seg_dst, post, hf, xs_prev, tm):
    t = hf.shape[0]
    tile = lambda i: (i, 0, 0)
    return pl.pallas_call(
        _dispatch_kernel,
        grid=(t // tm,),
        in_specs=[
            _seg_spec(tile), _seg_spec(tile), _seg_spec(tile),
            pl.BlockSpec((1, SUBLANES, tm), tile),
            pl.BlockSpec((tm, D_MODEL), lambda i: (i, 0)),
            pl.BlockSpec(memory_space=pl.ANY),
        ],
        out_specs=pl.BlockSpec(memory_space=pl.ANY),
        out_shape=jax.ShapeDtypeStruct(xs_prev.shape, F32),
        scratch_shapes=[pltpu.VMEM((2, TOP_K * tm * ROW_SLABS, LANES), F32),
                        pltpu.SemaphoreType.DMA((2,))],
        input_output_aliases={5: 0},
        compiler_params=_cparams(("arbitrary",)),
        name="moe_dispatch",
    )(cnt, seg_off, seg_dst, post, hf, xs_prev)


def _moe_kernel(be_ref, nu_ref, xs_ref, wg_ref, wl_ref, bg_ref, bl_ref, wd_ref, bd_ref,
                y_ref, xb_ref):
    del be_ref
    i = pl.program_id(0)
    tm = xb_ref.shape[0]

    @pl.when(i < nu_ref[0])
    def _():
        for c in range(ROW_SLABS):
            xb_ref[:, c * LANES:(c + 1) * LANES] = (
                xs_ref[pl.ds(c, tm, stride=ROW_SLABS), :].astype(BF16))
        x = xb_ref[...]
        g = _dot(x, wg_ref[0, 0]) + bg_ref[0, 0]
        lin = _dot(x, wl_ref[0, 0]) + bl_ref[0, 0]
        glu = jnp.minimum(g, SWIGLU_LIMIT)
        lin = jnp.clip(lin, -SWIGLU_LIMIT, SWIGLU_LIMIT)
        act = (lin + 1.0) * glu * _sigmoid(SWIGLU_ALPHA * glu)
        y = _dot(act.astype(BF16), wd_ref[0, 0]) + bd_ref[0, 0]
        for c in range(ROW_SLABS):
            y_ref[pl.ds(c, tm, stride=ROW_SLABS), :] = y[:, c * LANES:(c + 1) * LANES]

    @pl.when(i >= nu_ref[0])
    def _():
        y_ref[...] = jnp.zeros_like(y_ref)


def _moe_call(block_e, n_used, xs, wg, wl, bg, bl, wd, bd, layer, tm):
    n_tiles = xs.shape[0] // (tm * ROW_SLABS)
    row = lambda i, be, nu: (i, 0)
    wsel = lambda i, be, nu: (layer, be[i], 0, 0)
    grid_spec = pltpu.PrefetchScalarGridSpec(
        num_scalar_prefetch=2,
        grid=(n_tiles,),
        in_specs=[
            pl.BlockSpec((tm * ROW_SLABS, LANES), row),
            pl.BlockSpec((1, 1, D_MODEL, D_FF), wsel),
            pl.BlockSpec((1, 1, D_MODEL, D_FF), wsel),
            pl.BlockSpec((1, 1, 1, D_FF), wsel),
            pl.BlockSpec((1, 1, 1, D_FF), wsel),
            pl.BlockSpec((1, 1, D_FF, D_MODEL), wsel),
            pl.BlockSpec((1, 1, 1, D_MODEL), wsel),
        ],
        out_specs=pl.BlockSpec((tm * ROW_SLABS, LANES), row),
        scratch_shapes=[pltpu.VMEM((tm, D_MODEL), BF16)],
    )
    return pl.pallas_call(
        _moe_kernel,
        grid_spec=grid_spec,
        out_shape=jax.ShapeDtypeStruct(xs.shape, F32),
        compiler_params=_cparams(("arbitrary",)),
        name="moe_experts",
    )(block_e, n_used, xs, wg, wl, bg, bl, wd, bd)


def _combine_kernel(cnt_ref, off_ref, dst_ref, cnt_n_ref, off_n_ref, dst_n_ref,
                    x_ref, pos_ref, gate_ref, mod_ref, y_ref, o_ref, buf_ref, yb_ref, sems):
    i = pl.program_id(0)
    nt = pl.num_programs(0)
    tm = x_ref.shape[0]
    n_slots = TOP_K * tm
    slot = i % 2

    def fetch(tables, s):
        buf = buf_ref.at[s]
        _segment_copies(*tables, tm,
                        lambda o, d, n: pltpu.make_async_copy(_rows(y_ref, d, n), _rows(buf, o, n),
                                                              sems.at[s]))

    @pl.when(i == 0)
    def _():
        fetch((cnt_ref, off_ref, dst_ref), slot)

    @pl.when(i + 1 < nt)
    def _():
        fetch((cnt_n_ref, off_n_ref, dst_n_ref), 1 - slot)

    pltpu.make_async_copy(_rows(y_ref, 0, n_slots), buf_ref.at[slot], sems.at[slot]).wait()
    buf = buf_ref.at[slot]
    for c in range(ROW_SLABS):
        yb_ref[:, c * LANES:(c + 1) * LANES] = buf[pl.ds(c, n_slots, stride=ROW_SLABS), :].astype(BF16)
    pos = pos_ref[...]
    gates = gate_ref[...]
    j = lax.broadcasted_iota(I32, (tm, n_slots), 1)
    wsel = sum(jnp.where(pos[:, k:k + 1] == j, gates[:, k:k + 1], 0.0) for k in range(TOP_K))
    ffn = _dot(wsel.astype(BF16), yb_ref[...])
    o_ref[...] = x_ref[...] + mod_ref[0][:, 5 * D_MODEL:6 * D_MODEL] * ffn


def _combine_call(cnt, seg_off, seg_dst, x2, pos, gates, mod, y, seq, tm):
    t = x2.shape[0]
    nt = t // tm
    row = lambda i: (i, 0)
    tile = lambda i: (i, 0, 0)
    nxt = lambda i: (jnp.minimum(i + 1, nt - 1), 0, 0)
    return pl.pallas_call(
        _combine_kernel,
        grid=(nt,),
        in_specs=[
            _seg_spec(tile), _seg_spec(tile), _seg_spec(tile),
            _seg_spec(nxt), _seg_spec(nxt), _seg_spec(nxt),
            pl.BlockSpec((tm, D_MODEL), row),
            pl.BlockSpec((tm, LANES), row),
            pl.BlockSpec((tm, LANES), row),
            pl.BlockSpec((1, 1, 6 * D_MODEL), lambda i: ((i * tm) // seq, 0, 0)),
            pl.BlockSpec(memory_space=pl.ANY),
        ],
        out_specs=pl.BlockSpec((tm, D_MODEL), row),
        out_shape=jax.ShapeDtypeStruct((t, D_MODEL), F32),
        scratch_shapes=[pltpu.VMEM((2, TOP_K * tm * ROW_SLABS, LANES), F32),
                        pltpu.VMEM((TOP_K * tm, D_MODEL), BF16),
                        pltpu.SemaphoreType.DMA((2,))],
        compiler_params=_cparams(("arbitrary",)),
        name="moe_combine",
    )(cnt, seg_off, seg_dst, cnt, seg_off, seg_dst, x2, pos, gates, mod, y)


def _plan(cnt, tm_moe, n_tiles_moe):
    c = cnt[:, 0, :]
    totals = jnp.sum(c, axis=0)
    padded = (totals + tm_moe - 1) // tm_moe * tm_moe
    pad_end = jnp.cumsum(padded)
    pad_start = pad_end - padded
    seg_dst = pad_start[None, :] + jnp.cumsum(c, axis=0) - c
    seg_off = jnp.cumsum(c, axis=1) - c
    tile_start = jnp.arange(n_tiles_moe, dtype=I32) * tm_moe
    block_e = jnp.minimum(jnp.sum(tile_start[:, None] >= pad_end[None, :N_EXPERTS], axis=1),
                          N_EXPERTS - 1).astype(I32)
    n_used = (pad_end[N_EXPERTS - 1] // tm_moe).astype(I32).reshape(1)
    shape = cnt.shape
    return seg_off.astype(I32).reshape(shape), seg_dst.astype(I32).reshape(shape), block_e, n_used


def _pick(n, pref):
    for c in pref:
        if n % c == 0:
            return c
    raise ValueError(f"no tile size in {pref} divides {n}")


def _trunk(x, mods, prm):
    batch, seq, _ = x.shape
    t = batch * seq
    assert seq % RET_CHUNK == 0 and seq % ATT_BLOCK == 0
    tm_in = _pick(seq, (512, 256, 128))
    tm_route = _pick(seq, (256, 128))
    tm_step = _pick(seq, (2 * tm_route, tm_route))
    tm_moe = 512 if t * TOP_K >= 65536 * 4 else (256 if t * TOP_K >= 16384 else 128)
    n_tiles_moe = (t * TOP_K) // tm_moe + N_EXPERTS
    x2 = x.reshape(t, D_MODEL)
    xs = jnp.zeros((n_tiles_moe * tm_moe * ROW_SLABS, LANES), F32)
    tri = (jnp.arange(tm_route)[:, None] > jnp.arange(tm_route)[None, :]).astype(BF16)
    upper = (jnp.arange(LANES)[:, None] < jnp.arange(LANES)[None, :]).astype(BF16)
    depth = prm["w_in"].shape[0]
    for l in range(depth):
        mod = mods[l]
        qr, kr, vr, sg, qa, ka, va = _in_call(
            x2, mod, prm["norm_mix_g"][l], prm["w_in"], l, prm["g256"],
            prm["q_g"][l], prm["k_g"][l], seq, tm_in)
        o_r = _ret_call(qr, kr, vr, sg, prm["lgf"][l], prm["lgb"][l], prm["g128"], batch, seq)
        o_a = _attn_call(qa, ka, va, prm["sink"][l], prm["bias"], batch, seq)
        x2, hf, pos, post, gates, cnt = _route_call(
            x2, o_r, o_a, mod, prm["w_out"], l, prm["norm_ffn_g"][l],
            prm["rw"], prm["rb"][l], tri, upper, seq, tm_route, tm_step)
        seg_off, seg_dst, block_e, n_used = _plan(cnt, tm_moe, n_tiles_moe)
        xs = _dispatch_call(cnt, seg_off, seg_dst, post, hf, xs, tm_route)
        y = _moe_call(block_e, n_used, xs, prm["w_g"], prm["w_l"], prm["b_g"],
                      prm["b_l"], prm["w_dn"], prm["b_dn"], l, tm_moe)
        x2 = _combine_call(cnt, seg_off, seg_dst, x2, pos, gates, mod, y, seq, tm_route)
    return x2.reshape(batch, seq, D_MODEL)


def _prepare(ada_w, ada_b, norm_mix_g, w_in, ret_log_gamma_f, ret_log_gamma_b, q_norm_g,
             k_norm_g, attn_sink, w_out, norm_ffn_g, router_w, router_b, w_gu, b_gu, w_dn, b_dn):
    depth = w_in.shape[0]
    pad_e = LANES - N_EXPERTS
    rw = jnp.pad(router_w, ((0, 0), (0, 0), (0, pad_e)))
    rw_hi = rw.astype(BF16)
    rw_lo = (rw - rw_hi.astype(F32)).astype(BF16)
    rb = jnp.pad(router_b, ((0, 0), (0, pad_e)), constant_values=NEG_INF).reshape(depth, 1, LANES)
    lane = jnp.arange(2 * LANES)
    g256 = jnp.where((lane[:, None] // HEAD_DIM) == (lane[None, :] // HEAD_DIM),
                     1.0 / HEAD_DIM, 0.0).astype(BF16)
    g128 = g256[:LANES, :LANES]
    w_g, w_l = _deint_call(w_gu)
    return dict(
        norm_mix_g=norm_mix_g.reshape(depth, 1, D_MODEL),
        norm_ffn_g=norm_ffn_g.reshape(depth, 1, D_MODEL),
        w_in=w_in.astype(BF16),
        w_out=w_out.astype(BF16),
        lgf=ret_log_gamma_f.astype(F32),
        lgb=ret_log_gamma_b.astype(F32),
        q_g=jnp.tile(q_norm_g, (1, 2 * LANES // HEAD_DIM)).reshape(depth, 1, 2 * LANES),
        k_g=jnp.tile(k_norm_g, (1, LANES // HEAD_DIM)).reshape(depth, 1, LANES),
        sink=attn_sink.astype(F32) * LOG2E,
        rw=jnp.concatenate([rw_hi, rw_lo], axis=2), rb=rb,
        w_g=w_g,
        w_l=w_l,
        b_g=b_gu[..., 0::2].reshape(depth, N_EXPERTS, 1, D_FF),
        b_l=b_gu[..., 1::2].reshape(depth, N_EXPERTS, 1, D_FF),
        w_dn=w_dn.astype(BF16),
        b_dn=b_dn.reshape(depth, N_EXPERTS, 1, D_MODEL),
        g128=g128,
        g256=g256,
        bias=_attn_bias(),
    )


def kernel(x_prompt, x_sample, c_prompt, c_sample, ada_w, ada_b, norm_mix_g, w_in, ret_log_gamma_f, ret_log_gamma_b, q_norm_g, k_norm_g, attn_sink, w_out, norm_ffn_g, router_w, router_b, w_gu, b_gu, w_dn, b_dn):
    prm = _prepare(ada_w, ada_b, norm_mix_g, w_in, ret_log_gamma_f, ret_log_gamma_b, q_norm_g,
                   k_norm_g, attn_sink, w_out, norm_ffn_g, router_w, router_b, w_gu, b_gu,
                   w_dn, b_dn)
    bp, bs = c_prompt.shape[0], c_sample.shape[0]
    n_c = bp + bs
    n_pad = -n_c % SUBLANES
    c_all = jnp.concatenate([c_prompt, c_sample, jnp.zeros((n_pad, D_MODEL), F32)], axis=0)
    mods = _ada_call(c_all, ada_w, ada_b)
    depth = ada_w.shape[0]
    mods_p = mods[:, :bp].reshape(depth, bp, 1, 6 * D_MODEL)
    mods_s = mods[:, bp:n_c].reshape(depth, bs, 1, 6 * D_MODEL)
    y_prompt = _trunk(x_prompt, mods_p, prm)
    y_sample = _trunk(x_sample, mods_s, prm)
    return (y_prompt, y_sample)
```

```python
import functools

import jax
import jax.numpy as jnp
from jax import lax
from jax.experimental import pallas as pl
from jax.experimental.pallas import tpu as pltpu

F32 = jnp.float32
BF16 = jnp.bfloat16
I32 = jnp.int32

D_MODEL = 1024
RET_HEADS = 8
HEAD_DIM = 64
RET_WIDTH = RET_HEADS * HEAD_DIM
RET_CHUNK = 128
ATT_HEADS = 8
ATT_KV_HEADS = 2
ATT_WIDTH = ATT_HEADS * HEAD_DIM
ATT_KV_WIDTH = ATT_KV_HEADS * HEAD_DIM
WINDOW = 128
ATT_BLOCK = 128
IN_WIDTH = 4 * RET_WIDTH + ATT_WIDTH + 2 * ATT_KV_WIDTH
N_EXPERTS = 32
TOP_K = 4
D_FF = D_MODEL
SWIGLU_LIMIT = 7.0
SWIGLU_ALPHA = 1.702
RMS_EPS = 1e-6
NEG_INF = -1e30

LANES = 128
SUBLANES = 8
ROW_SLABS = D_MODEL // LANES
VMEM_LIMIT = 56 * 1024 * 1024


def _cparams(sem):
    return pltpu.CompilerParams(dimension_semantics=sem, vmem_limit_bytes=VMEM_LIMIT)


def _sigmoid(x):
    return 1.0 / (1.0 + jnp.exp(-x))


def _split_bf16(x):
    hi = x.astype(BF16)
    lo = (x - hi.astype(F32)).astype(BF16)
    return hi, lo


def _dot(a, b):
    return jnp.dot(a, b, preferred_element_type=F32)


def _dot_nt(a, b):
    return lax.dot_general(a, b, (((1,), (1,)), ((), ())), preferred_element_type=F32)


def _group_mean(x2, g_ref):
    return _dot(x2.astype(BF16), g_ref[...])


def _ada_kernel(c_ref, w_ref, b_ref, o_ref):
    c = c_ref[...]
    a = c * _sigmoid(c)
    a_hi, a_lo = _split_bf16(a)
    w_hi, w_lo = _split_bf16(w_ref[0])
    acc = _dot(a_hi, w_hi) + _dot(a_lo, w_hi) + _dot(a_hi, w_lo)
    o_ref[0] = acc + b_ref[0]


def _ada_call(c, ada_w, ada_b):
    depth = ada_w.shape[0]
    bp = c.shape[0]
    tn = 1536
    n_out = ada_w.shape[2]
    return pl.pallas_call(
        _ada_kernel,
        grid=(depth, n_out // tn),
        in_specs=[
            pl.BlockSpec((bp, D_MODEL), lambda l, j: (0, 0)),
            pl.BlockSpec((1, D_MODEL, tn), lambda l, j: (l, 0, j)),
            pl.BlockSpec((1, 1, tn), lambda l, j: (l, 0, j)),
        ],
        out_specs=pl.BlockSpec((1, bp, tn), lambda l, j: (l, 0, j)),
        out_shape=jax.ShapeDtypeStruct((depth, bp, n_out), F32),
        compiler_params=_cparams(("arbitrary", "arbitrary")),
        name="ada_mod",
    )(c, ada_w, ada_b.reshape(depth, 1, n_out))


DEINT_COLS = 256


def _deint_kernel(w_ref, p_ref, g_ref, l_ref):
    perm = p_ref[...]
    half = DEINT_COLS // 2
    for j in range(w_ref.shape[2] // DEINT_COLS):
        w = w_ref[0, :, j * DEINT_COLS:(j + 1) * DEINT_COLS].astype(BF16)
        r = _dot(w, perm)
        g_ref[0, :, j * half:(j + 1) * half] = r[:, :half].astype(BF16)
        l_ref[0, :, j * half:(j + 1) * half] = r[:, half:].astype(BF16)


def _deint_call(w_gu):
    depth, n_e, d_in, d_out2 = w_gu.shape
    w = w_gu.reshape(depth * n_e, d_in, d_out2)
    tk = 512
    src = jnp.arange(DEINT_COLS)
    dst = jnp.where(src % 2 == 0, src // 2, DEINT_COLS // 2 + src // 2)
    perm = (dst[:, None] == jnp.arange(DEINT_COLS)[None, :]).astype(BF16)
    out = jax.ShapeDtypeStruct((depth * n_e, d_in, d_out2 // 2), BF16)
    blk = lambda e, i: (e, i, 0)
    w_g, w_l = pl.pallas_call(
        _deint_kernel,
        grid=(depth * n_e, d_in // tk),
        in_specs=[pl.BlockSpec((1, tk, d_out2), blk),
                  pl.BlockSpec((DEINT_COLS, DEINT_COLS), lambda e, i: (0, 0))],
        out_specs=[pl.BlockSpec((1, tk, d_out2 // 2), blk)] * 2,
        out_shape=[out, out],
        compiler_params=_cparams(("parallel", "parallel")),
        name="expert_weight_split",
    )(w, perm)
    shape = (depth, n_e, d_in, d_out2 // 2)
    return w_g.reshape(shape), w_l.reshape(shape)


def _modulate(x, gain, shift, scale):
    ms = jnp.mean(x * x, axis=-1, keepdims=True)
    xn = x * lax.rsqrt(ms + RMS_EPS)
    return xn * gain * (1.0 + scale) + shift


def _in_kernel(x_ref, mod_ref, gain_ref, w_ref, g_ref, qg_ref, kg_ref,
               qr_ref, kr_ref, vr_ref, sg_ref, qa_ref, ka_ref, va_ref):
    mod = mod_ref[0]
    h = _modulate(x_ref[...], gain_ref[...], mod[:, 0:D_MODEL], mod[:, D_MODEL:2 * D_MODEL])
    proj = _dot(h.astype(BF16), w_ref[0])
    w = RET_WIDTH
    qr_ref[...] = proj[:, 0:w].astype(BF16)
    kr_ref[...] = (proj[:, w:2 * w] * (HEAD_DIM ** -0.5)).astype(BF16)
    vr_ref[...] = proj[:, 2 * w:3 * w].astype(BF16)
    gr = proj[:, 3 * w:4 * w]
    sg_ref[...] = (gr * _sigmoid(gr)).astype(BF16)
    base = 4 * w
    qg = qg_ref[...] * (HEAD_DIM ** -0.5 * LOG2E)
    wide = g_ref.shape[0]
    for s in range(ATT_WIDTH // wide):
        q = proj[:, base + s * wide: base + (s + 1) * wide]
        msq = _group_mean(q * q, g_ref)
        qa_ref[:, s * wide:(s + 1) * wide] = (q * lax.rsqrt(msq + RMS_EPS) * qg).astype(BF16)
    kb = base + ATT_WIDTH
    k = proj[:, kb:kb + LANES]
    msk = _group_mean(k * k, g_ref.at[0:LANES, 0:LANES])
    kn = k * lax.rsqrt(msk + RMS_EPS) * kg_ref[...]
    v = proj[:, kb + LANES:kb + 2 * LANES]
    lane = lax.broadcasted_iota(I32, kn.shape, 1)
    first = lane < HEAD_DIM
    for src, dst in ((kn, ka_ref), (v, va_ref)):
        rot = pltpu.roll(src, HEAD_DIM, 1)
        dst[:, 0:LANES] = jnp.where(first, src, rot).astype(BF16)
        dst[:, LANES:2 * LANES] = jnp.where(first, rot, src).astype(BF16)


def _in_call(x2, mod, gain, w_in, layer, g256, qg, kg, seq, tm):
    t = x2.shape[0]
    row = lambda i: (i, 0)
    const = lambda i: (0, 0)
    outs = [
        jax.ShapeDtypeStruct((t, RET_WIDTH), BF16),
        jax.ShapeDtypeStruct((t, RET_WIDTH), BF16),
        jax.ShapeDtypeStruct((t, RET_WIDTH), BF16),
        jax.ShapeDtypeStruct((t, RET_WIDTH), BF16),
        jax.ShapeDtypeStruct((t, ATT_WIDTH), BF16),
        jax.ShapeDtypeStruct((t, 2 * LANES), BF16),
        jax.ShapeDtypeStruct((t, 2 * LANES), BF16),
    ]
    return pl.pallas_call(
        _in_kernel,
        grid=(t // tm,),
        in_specs=[
            pl.BlockSpec((tm, D_MODEL), row),
            pl.BlockSpec((1, 1, 6 * D_MODEL), lambda i: ((i * tm) // seq, 0, 0)),
            pl.BlockSpec((1, D_MODEL), const),
            pl.BlockSpec((1, D_MODEL, IN_WIDTH), lambda i: (layer, 0, 0)),
            pl.BlockSpec(g256.shape, const),
            pl.BlockSpec((1, g256.shape[0]), const),
            pl.BlockSpec((1, LANES), const),
        ],
        out_specs=[pl.BlockSpec((tm, o.shape[1]), row) for o in outs],
        out_shape=outs,
        compiler_params=_cparams(("parallel",)),
        name="in_proj",
    )(x2, mod, gain, w_in, g256, qg, kg)


def _ret_kernel(lgf_ref, lgb_ref, q_ref, k_ref, v_ref, sg_ref, g_ref, o_ref,
                uf_ref, ub_ref, sf_ref, sb_ref, s_ref, acc_ref):
    p = pl.program_id(1)
    c = RET_CHUNK
    n_chunks = q_ref.shape[0] // c
    lf0, lf1 = lgf_ref[2 * p], lgf_ref[2 * p + 1]
    lb0, lb1 = lgb_ref[2 * p], lgb_ref[2 * p + 1]
    row = lax.broadcasted_iota(I32, (c, LANES), 0)
    lane = lax.broadcasted_iota(I32, (c, LANES), 1)
    pos = row.astype(F32)
    first = lane < HEAD_DIM
    lf_lane = jnp.where(first, lf0, lf1)
    lb_lane = jnp.where(first, lb0, lb1)
    qd_f = jnp.exp(lf_lane * (pos + 1.0))
    kd_f = jnp.exp(lf_lane * (c - 1.0 - pos))
    qd_b = jnp.exp(lb_lane * (c - pos))
    kd_b = jnp.exp(lb_lane * pos)
    cd_f = jnp.exp(lf_lane * float(c))
    cd_b = jnp.exp(lb_lane * float(c))
    same_head = (row < HEAD_DIM) == first
    row2 = lax.broadcasted_iota(I32, (c, 2 * c), 0)
    col2 = lax.broadcasted_iota(I32, (c, 2 * c), 1)
    head1 = col2 >= c
    diff = (row2 - jnp.where(head1, col2 - c, col2)).astype(F32)
    lf_c = jnp.where(head1, lf1, lf0)
    lb_c = jnp.where(head1, lb1, lb0)
    decay = jnp.where(diff >= 0, jnp.exp(lf_c * jnp.maximum(diff, 0.0)),
                      jnp.exp(lb_c * jnp.maximum(-diff, 0.0)))

    unroll = _pick(n_chunks, (16, 8, 4, 2, 1))

    def updates(g, carry):
        for u in range(unroll):
            n = g * unroll + u
            off = pl.multiple_of(n * c, c)
            k = k_ref[pl.ds(off, c), :].astype(F32)
            vt = v_ref[pl.ds(off, c), :].astype(F32).T.astype(BF16)
            uf_ref[n] = jnp.where(same_head, _dot(vt, (k * kd_f).astype(BF16)), 0.0)
            ub_ref[n] = jnp.where(same_head, _dot(vt, (k * kd_b).astype(BF16)), 0.0)
        return carry

    lax.fori_loop(0, n_chunks // unroll, updates, 0)

    def scan_f(n, state):
        sf_ref[n] = state.astype(BF16)
        return state * cd_f + uf_ref[n]

    def scan_b(i, state):
        n = n_chunks - 1 - i
        sb_ref[n] = state.astype(BF16)
        return state * cd_b + ub_ref[n]

    zero_state = jnp.zeros((LANES, LANES), F32)
    lax.fori_loop(0, n_chunks, scan_f, zero_state)
    lax.fori_loop(0, n_chunks, scan_b, zero_state)

    group = s_ref.shape[0]
    zero = jnp.zeros((c, LANES), BF16)

    def split_heads(x):
        return jnp.concatenate([jnp.where(first, x, zero), jnp.where(first, zero, x)], axis=0)

    def outputs(g, carry):
        offs = [pl.multiple_of((g * group + u) * c, c) for u in range(group)]

        def scores(u):
            s_ref[u] = _dot_nt(q_ref[pl.ds(offs[u], c), :],
                               split_heads(k_ref[pl.ds(offs[u], c), :]))

        def mix(u):
            n = g * group + u
            q = q_ref[pl.ds(offs[u], c), :].astype(F32)
            acc_ref[u] = (_dot((s_ref[u] * decay).astype(BF16), split_heads(v_ref[pl.ds(offs[u], c), :]))
                          + _dot_nt((q * qd_f).astype(BF16), sf_ref[n])
                          + _dot_nt((q * qd_b).astype(BF16), sb_ref[n]))

        def finish(u):
            o = acc_ref[u]
            ms = _group_mean(o * o, g_ref)
            on = o * lax.rsqrt(ms + RMS_EPS)
            o_ref[pl.ds(offs[u], c), :] = (on * sg_ref[pl.ds(offs[u], c), :].astype(F32)).astype(BF16)

        for stage in (scores, mix, finish):
            for u in range(group):
                stage(u)
        return carry

    lax.fori_loop(0, n_chunks // group, outputs, 0)


def _ret_call(qr, kr, vr, sg, lgf, lgb, g128, batch, seq):
    t = qr.shape[0]
    blk = lambda b, p: (b, p)
    smem = pl.BlockSpec(memory_space=pltpu.SMEM)
    n_chunks = seq // RET_CHUNK
    group = _pick(n_chunks, (16, 8, 4, 2, 1))
    return pl.pallas_call(
        _ret_kernel,
        grid=(batch, RET_HEADS // 2),
        in_specs=[smem, smem] + [pl.BlockSpec((seq, LANES), blk)] * 4
                 + [pl.BlockSpec((LANES, LANES), lambda b, p: (0, 0))],
        out_specs=pl.BlockSpec((seq, LANES), blk),
        out_shape=jax.ShapeDtypeStruct((t, RET_WIDTH), BF16),
        scratch_shapes=[pltpu.VMEM((n_chunks, LANES, LANES), F32),
                        pltpu.VMEM((n_chunks, LANES, LANES), F32),
                        pltpu.VMEM((n_chunks, LANES, LANES), BF16),
                        pltpu.VMEM((n_chunks, LANES, LANES), BF16),
                        pltpu.VMEM((group, RET_CHUNK, 2 * RET_CHUNK), F32),
                        pltpu.VMEM((group, RET_CHUNK, LANES), F32)],
        compiler_params=_cparams(("parallel", "parallel")),
        name="retention",
    )(lgf, lgb, qr, kr, vr, sg, g128)


ATT_SUB = 2


ATT_STRIP = 32


def _attn_kernel(sink_ref, q_ref, kp_ref, kc_ref, kn_ref, vp_ref, vc_ref, vn_ref,
                 bias_a_ref, bias_b_ref, o_ref, kcat_ref, vcat_ref, s_ref, p_ref, inv_ref):
    blk = ATT_BLOCK
    rows = (ATT_SUB + 2) * blk
    first = lax.broadcasted_iota(I32, (rows, LANES), 1) < HEAD_DIM
    out_first = lax.broadcasted_iota(I32, (ATT_STRIP, LANES), 1) < HEAD_DIM
    kk = jnp.concatenate([kp_ref[...], kc_ref[...], kn_ref[...]], axis=0)
    vv = jnp.concatenate([vp_ref[...], vc_ref[...], vn_ref[...]], axis=0)
    zero = jnp.zeros((rows, LANES), BF16)
    pairs_per_kv = ATT_HEADS // ATT_KV_HEADS // 2
    combos = []
    for kv in range(ATT_KV_HEADS):
        kd = kk[:, kv * LANES:(kv + 1) * LANES]
        vd = vv[:, kv * LANES:(kv + 1) * LANES]
        k_lo, k_hi = jnp.where(first, kd, zero), jnp.where(first, zero, kd)
        v_lo, v_hi = jnp.where(first, vd, zero), jnp.where(first, zero, vd)
        for sub in range(ATT_SUB):
            band = slice(sub * blk, (sub + 3) * blk)
            g = kv * ATT_SUB + sub
            kcat_ref[g] = jnp.concatenate([k_lo[band], k_hi[band]], axis=0)
            vcat_ref[g] = jnp.concatenate([v_lo[band], v_hi[band]], axis=0)
            bias_ref = bias_a_ref if sub == 0 else bias_b_ref
            qrows = slice(sub * blk, (sub + 1) * blk)
            for pp in range(pairs_per_kv):
                hp = kv * pairs_per_kv + pp
                combos.append((len(combos), qrows, hp, bias_ref, g))
    def scores(combo):
        c, qrows, hp, _, g = combo
        s_ref[c] = _dot_nt(q_ref[qrows, hp * LANES:(hp + 1) * LANES], kcat_ref[g])

    def softmax(combo):
        c, _, hp, bias_ref, _ = combo
        for r0 in range(0, blk, ATT_STRIP):
            strip = slice(r0, r0 + ATT_STRIP)
            inv = []
            for hh in range(2):
                h = 2 * hp + hh
                cols = slice(hh * 3 * blk, (hh + 1) * 3 * blk)
                sh = s_ref[c, strip, cols] + bias_ref[0, h, strip, :]
                sink = sink_ref[h]
                m = jnp.maximum(jnp.max(sh, axis=-1, keepdims=True), sink)
                e = jnp.exp2(sh - m)
                den = jnp.sum(e, axis=-1, keepdims=True) + jnp.exp2(sink - m)
                p_ref[c, strip, cols] = e.astype(BF16)
                inv.append(1.0 / den)
            inv_ref[c, strip, :] = jnp.where(out_first, inv[0], inv[1])

    def values(combo):
        c, qrows, hp, _, g = combo
        o = _dot(p_ref[c], vcat_ref[g]) * inv_ref[c]
        o_ref[qrows, hp * LANES:(hp + 1) * LANES] = o.astype(BF16)

    scores(combos[0])
    for n, combo in enumerate(combos):
        if n + 1 < len(combos):
            scores(combos[n + 1])
        softmax(combo)
        if n >= 1:
            values(combos[n - 1])
    values(combos[-1])


LOG2E = 1.4426950408889634


def _attn_bias():
    blk = ATT_BLOCK
    i = jnp.arange(blk, dtype=F32)[:, None]
    jj = jnp.arange(3 * blk, dtype=F32)[None, :]
    rel = jnp.abs(i + blk - jj)
    slopes = 2.0 ** (-8.0 * (jnp.arange(ATT_HEADS, dtype=F32) + 1.0) / ATT_HEADS)
    base = jnp.where(rel <= WINDOW, -slopes[:, None, None] * rel * LOG2E, NEG_INF)
    no_prev = jnp.where(jj < blk, NEG_INF, base)
    no_next = jnp.where(jj >= 2 * blk, NEG_INF, base)
    return jnp.stack([no_prev, base, no_next])


def _attn_call(qa, ka, va, sink, bias, batch, seq):
    t = qa.shape[0]
    blk = ATT_BLOCK
    assert seq % (ATT_SUB * blk) == 0
    nb = seq // blk
    ns = nb // ATT_SUB
    cur = lambda b, j: (b * ns + j, 0)
    prev = lambda b, j: (b * nb + jnp.maximum(ATT_SUB * j - 1, 0), 0)
    nxt = lambda b, j: (b * nb + jnp.minimum(ATT_SUB * (j + 1), nb - 1), 0)
    kv_specs = [pl.BlockSpec((blk, 2 * LANES), prev),
                pl.BlockSpec((ATT_SUB * blk, 2 * LANES), cur),
                pl.BlockSpec((blk, 2 * LANES), nxt)]
    bias_block = (1, ATT_HEADS, blk, 3 * blk)
    n_combo = ATT_SUB * ATT_HEADS // 2
    return pl.pallas_call(
        _attn_kernel,
        grid=(batch, ns),
        in_specs=[pl.BlockSpec(memory_space=pltpu.SMEM),
                  pl.BlockSpec((ATT_SUB * blk, ATT_WIDTH), cur)] + kv_specs + kv_specs
                 + [pl.BlockSpec(bias_block, lambda b, j: (jnp.where(j == 0, 0, 1), 0, 0, 0)),
                    pl.BlockSpec(bias_block, lambda b, j: (jnp.where(j == ns - 1, 2, 1), 0, 0, 0))],
        out_specs=pl.BlockSpec((ATT_SUB * blk, ATT_WIDTH), cur),
        out_shape=jax.ShapeDtypeStruct((t, ATT_WIDTH), BF16),
        scratch_shapes=[pltpu.VMEM((ATT_KV_HEADS * ATT_SUB, 6 * blk, LANES), BF16),
                        pltpu.VMEM((ATT_KV_HEADS * ATT_SUB, 6 * blk, LANES), BF16),
                        pltpu.VMEM((n_combo, blk, 6 * blk), F32),
                        pltpu.VMEM((n_combo, blk, 6 * blk), BF16),
                        pltpu.VMEM((n_combo, blk, LANES), F32)],
        compiler_params=_cparams(("parallel", "arbitrary")),
        name="window_attn",
    )(sink, qa, ka, ka, ka, va, va, va, bias, bias)


def _route_kernel(x_ref, or_ref, oa_ref, mod_ref, wo_ref, gain_ref, rw_ref,
                  rb_ref, tri_ref, upper_ref, xo_ref, hf_ref, pos_ref, post_ref, gate_ref, cnt_ref,
                  lg_ref, sel_ref, idx_ref, bef_ref, posf_ref):
    sub = tri_ref.shape[0]
    tiles = range(x_ref.shape[0] // sub)
    rows = [slice(h * sub, (h + 1) * sub) for h in tiles]
    for h in tiles:
        _route_logits(x_ref.at[rows[h]], or_ref.at[rows[h]], oa_ref.at[rows[h]], mod_ref, wo_ref,
                      gain_ref, rw_ref, rb_ref, xo_ref.at[rows[h]], hf_ref.at[rows[h]], lg_ref.at[h])
    counts = [_route_topk(lg_ref.at[h], gate_ref.at[rows[h]], idx_ref.at[h], sel_ref.at[h])
              for h in tiles]
    for h in tiles:
        _route_slots(counts[h], tri_ref, upper_ref, sel_ref.at[h], idx_ref.at[h], bef_ref.at[h],
                     posf_ref.at[h], pos_ref.at[rows[h]], post_ref.at[h], cnt_ref.at[h])


ROUTE_STRIP = 128


def _route_logits(x_ref, or_ref, oa_ref, mod_ref, wo_ref, gain_ref, rw_ref, rb_ref,
                  xo_ref, hf_ref, lg_ref):
    mod = mod_ref[0]
    d = D_MODEL
    mix = _dot(or_ref[...], wo_ref[0, 0:RET_WIDTH, :]) + _dot(oa_ref[...], wo_ref[0, RET_WIDTH:, :])
    x1 = x_ref[...] + mod[:, 2 * d:3 * d] * mix
    xo_ref[...] = x1
    hf = _modulate(x1, gain_ref[...], mod[:, 3 * d:4 * d], mod[:, 4 * d:5 * d])
    h_hi, h_lo = _split_bf16(hf)
    hf_ref[...] = h_hi
    both = _dot(h_hi, rw_ref[0])
    lg_ref[...] = (both[:, 0:LANES] + both[:, LANES:2 * LANES]
                   + _dot(h_lo, rw_ref[0, :, 0:LANES]) + rb_ref[...])


def _route_strips(n_rows):
    lane = lax.broadcasted_iota(I32, (ROUTE_STRIP, LANES), 1)
    return lane, lane.astype(F32), [slice(r0, r0 + ROUTE_STRIP) for r0 in range(0, n_rows, ROUTE_STRIP)]


def _route_topk(lg_ref, gate_ref, idx_ref, sel_ref):
    lane, lane_f, strips = _route_strips(lg_ref.shape[0])
    counts = jnp.zeros((1, LANES), F32)
    for strip in strips:
        work = lg_ref[strip, :]
        sel = jnp.zeros(work.shape, F32)
        idx_m = jnp.zeros(work.shape, F32)
        vals = []
        for k in range(TOP_K):
            m = jnp.max(work, axis=-1, keepdims=True)
            idx = jnp.min(jnp.where(work == m, lane_f, float(LANES)), axis=-1, keepdims=True)
            hot = lane_f == idx
            work = jnp.where(hot, -jnp.inf, work)
            sel = sel + jnp.where(hot, 1.0, 0.0)
            idx_m = jnp.where(lane == k, idx, idx_m)
            vals.append(m)
        exps = [jnp.exp(v - vals[0]) for v in vals]
        den = exps[0] + exps[1] + exps[2] + exps[3]
        gates = jnp.zeros(work.shape, F32)
        for k in range(TOP_K):
            gates = jnp.where(lane == k, exps[k] / den, gates)
        gate_ref[strip, :] = gates
        idx_ref[strip, :] = idx_m
        sel_ref[strip, :] = sel.astype(BF16)
        counts = counts + jnp.sum(sel, axis=0, keepdims=True)
    return counts


def _route_slots(counts, tri_ref, upper_ref, sel_ref, idx_ref, bef_ref, posf_ref, pos_ref, post_ref,
                 cnt_ref):
    lane, lane_f, strips = _route_strips(sel_ref.shape[0])
    bef_ref[...] = _dot(tri_ref[...], sel_ref[...])
    seg_off = _dot(jnp.broadcast_to(counts, (SUBLANES, LANES)).astype(BF16), upper_ref[...])[0:1]
    for strip in strips:
        slot = bef_ref[strip, :] + seg_off
        idx_m = idx_ref[strip, :]
        pos = jnp.zeros(slot.shape, F32)
        for k in range(TOP_K):
            hot = lane_f == idx_m[:, k:k + 1]
            pk = jnp.sum(jnp.where(hot, slot, 0.0), axis=-1, keepdims=True)
            pos = jnp.where(lane == k, pk, pos)
        posf_ref[strip, :] = pos
        pos_ref[strip, :] = pos.astype(I32)
    post_ref[...] = posf_ref[...].T[0:SUBLANES, :].astype(I32)
    cnt_ref[...] = counts.astype(I32)


def _route_call(x2, o_r, o_a, mod, w_out, layer, gain, rw, rb, tri, upper, seq, sub, tm):
    t = x2.shape[0]
    nt = t // tm
    n_sub = tm // sub
    row = lambda i: (i, 0)
    const = lambda i: (0, 0)
    outs = [
        jax.ShapeDtypeStruct((t, D_MODEL), F32),
        jax.ShapeDtypeStruct((t, D_MODEL), BF16),
        jax.ShapeDtypeStruct((t, LANES), I32),
        jax.ShapeDtypeStruct((t // sub, SUBLANES, sub), I32),
        jax.ShapeDtypeStruct((t, LANES), F32),
        jax.ShapeDtypeStruct((t // sub, 1, LANES), I32),
    ]
    out_specs = [
        pl.BlockSpec((tm, D_MODEL), row),
        pl.BlockSpec((tm, D_MODEL), row),
        pl.BlockSpec((tm, LANES), row),
        pl.BlockSpec((n_sub, SUBLANES, sub), lambda i: (i, 0, 0)),
        pl.BlockSpec((tm, LANES), row),
        pl.BlockSpec((n_sub, 1, LANES), lambda i: (i, 0, 0)),
    ]
    return pl.pallas_call(
        _route_kernel,
        grid=(nt,),
        in_specs=[
            pl.BlockSpec((tm, D_MODEL), row),
            pl.BlockSpec((tm, RET_WIDTH), row),
            pl.BlockSpec((tm, ATT_WIDTH), row),
            pl.BlockSpec((1, 1, 6 * D_MODEL), lambda i: ((i * tm) // seq, 0, 0)),
            pl.BlockSpec((1, D_MODEL, D_MODEL), lambda i: (layer, 0, 0)),
            pl.BlockSpec((1, D_MODEL), const),
            pl.BlockSpec((1, D_MODEL, 2 * LANES), lambda i: (layer, 0, 0)),
            pl.BlockSpec((1, LANES), const),
            pl.BlockSpec((sub, sub), const),
            pl.BlockSpec((LANES, LANES), const),
        ],
        out_specs=out_specs,
        out_shape=outs,
        scratch_shapes=[pltpu.VMEM((n_sub, sub, LANES), F32),
                        pltpu.VMEM((n_sub, sub, LANES), BF16),
                        pltpu.VMEM((n_sub, sub, LANES), F32),
                        pltpu.VMEM((n_sub, sub, LANES), F32),
                        pltpu.VMEM((n_sub, sub, LANES), F32)],
        compiler_params=_cparams(("parallel",)),
        name="out_proj_route",
    )(x2, o_r, o_a, mod, w_out, gain, rw, rb, tri, upper)


def _rows(ref, row, n):
    return ref.at[pl.ds(pl.multiple_of(row * ROW_SLABS, SUBLANES), n * ROW_SLABS), :]


def _segment_copies(cnt_ref, off_ref, dst_ref, tm, make_copy):
    unroll = 4

    def expert_group(g, carry):
        for u in range(unroll):
            e = g * unroll + u
            n = cnt_ref[0, 0, e]
            off = off_ref[0, 0, e]
            dst = dst_ref[0, 0, e]
            piece = tm
            while piece >= 1:
                done = n & (-2 * piece)

                @pl.when((n & piece) != 0)
                def _(piece=piece, done=done):
                    make_copy(off + done, dst + done, piece).start()
                piece //= 2
        return carry

    lax.fori_loop(0, N_EXPERTS // unroll, expert_group, 0)


def _dispatch_kernel(cnt_ref, off_ref, dst_ref, post_ref, hf_ref, xs_in_ref, xs_ref, stage_ref, sems):
    del xs_in_ref
    i = pl.program_id(0)
    tm = hf_ref.shape[0]
    n_slots = TOP_K * tm
    slot = i % 2
    stage = stage_ref.at[slot]
    post = post_ref[0]
    j = lax.broadcasted_iota(I32, (n_slots, tm), 0)
    perm = sum(jnp.where(post[k:k + 1, :] == j, 1.0, 0.0) for k in range(TOP_K))
    srt = _dot(perm.astype(BF16), hf_ref[...])
    for c in range(ROW_SLABS):
        stage[pl.ds(c, n_slots, stride=ROW_SLABS), :] = srt[:, c * LANES:(c + 1) * LANES]
    _segment_copies(cnt_ref, off_ref, dst_ref, tm,
                    lambda o, d, n: pltpu.make_async_copy(_rows(stage, o, n), _rows(xs_ref, d, n),
                                                          sems.at[slot]))

    def wait_tile(s):
        pltpu.make_async_copy(stage_ref.at[s], _rows(xs_ref, 0, n_slots), sems.at[s]).wait()

    @pl.when(i > 0)
    def _():
        wait_tile(1 - slot)

    @pl.when(i == pl.num_programs(0) - 1)
    def _():
        wait_tile(slot)


def _seg_spec(index_map):
    return pl.BlockSpec((1, 1, LANES), index_map, memory_space=pltpu.SMEM)


def _dispatch_call(cnt, seg_off, seg_dst, post, hf, xs_prev, tm):
    t = hf.shape[0]
    tile = lambda i: (i, 0, 0)
    return pl.pallas_call(
        _dispatch_kernel,
        grid=(t // tm,),
        in_specs=[
            _seg_spec(tile), _seg_spec(tile), _seg_spec(tile),
            pl.BlockSpec((1, SUBLANES, tm), tile),
            pl.BlockSpec((tm, D_MODEL), lambda i: (i, 0)),
            pl.BlockSpec(memory_space=pl.ANY),
        ],
        out_specs=pl.BlockSpec(memory_space=pl.ANY),
        out_shape=jax.ShapeDtypeStruct(xs_prev.shape, F32),
        scratch_shapes=[pltpu.VMEM((2, TOP_K * tm * ROW_SLABS, LANES), F32),
                        pltpu.SemaphoreType.DMA((2,))],
        input_output_aliases={5: 0},
        compiler_params=_cparams(("arbitrary",)),
        name="moe_dispatch",
    )(cnt, seg_off, seg_dst, post, hf, xs_prev)


def _moe_kernel(be_ref, nu_ref, xs_ref, wg_ref, wl_ref, bg_ref, bl_ref, wd_ref, bd_ref,
                y_ref, xb_ref):
    del be_ref
    i = pl.program_id(0)
    tm = xb_ref.shape[0]

    @pl.when(i < nu_ref[0])
    def _():
        for c in range(ROW_SLABS):
            xb_ref[:, c * LANES:(c + 1) * LANES] = (
                xs_ref[pl.ds(c, tm, stride=ROW_SLABS), :].astype(BF16))
        x = xb_ref[...]
        g = _dot(x, wg_ref[0, 0]) + bg_ref[0, 0]
        lin = _dot(x, wl_ref[0, 0]) + bl_ref[0, 0]
        glu = jnp.minimum(g, SWIGLU_LIMIT)
        lin = jnp.clip(lin, -SWIGLU_LIMIT, SWIGLU_LIMIT)
        act = (lin + 1.0) * glu * _sigmoid(SWIGLU_ALPHA * glu)
        y = _dot(act.astype(BF16), wd_ref[0, 0]) + bd_ref[0, 0]
        for c in range(ROW_SLABS):
            y_ref[pl.ds(c, tm, stride=ROW_SLABS), :] = y[:, c * LANES:(c + 1) * LANES]

    @pl.when(i >= nu_ref[0])
    def _():
        y_ref[...] = jnp.zeros_like(y_ref)


def _moe_call(block_e, n_used, xs, wg, wl, bg, bl, wd, bd, layer, tm):
    n_tiles = xs.shape[0] // (tm * ROW_SLABS)
    row = lambda i, be, nu: (i, 0)
    wsel = lambda i, be, nu: (layer, be[i], 0, 0)
    grid_spec = pltpu.PrefetchScalarGridSpec(
        num_scalar_prefetch=2,
        grid=(n_tiles,),
        in_specs=[
            pl.BlockSpec((tm * ROW_SLABS, LANES), row),
            pl.BlockSpec((1, 1, D_MODEL, D_FF), wsel),
            pl.BlockSpec((1, 1, D_MODEL, D_FF), wsel),
            pl.BlockSpec((1, 1, 1, D_FF), wsel),
            pl.BlockSpec((1, 1, 1, D_FF), wsel),
            pl.BlockSpec((1, 1, D_FF, D_MODEL), wsel),
            pl.BlockSpec((1, 1, 1, D_MODEL), wsel),
        ],
        out_specs=pl.BlockSpec((tm * ROW_SLABS, LANES), row),
        scratch_shapes=[pltpu.VMEM((tm, D_MODEL), BF16)],
    )
    return pl.pallas_call(
        _moe_kernel,
        grid_spec=grid_spec,
        out_shape=jax.ShapeDtypeStruct(xs.shape, F32),
        compiler_params=_cparams(("arbitrary",)),
        name="moe_experts",
    )(block_e, n_used, xs, wg, wl, bg, bl, wd, bd)


def _combine_kernel(cnt_ref, off_ref, dst_ref, cnt_n_ref, off_n_ref, dst_n_ref,
                    x_ref, pos_ref, gate_ref, mod_ref, y_ref, o_ref, buf_ref, yb_ref, sems):
    i = pl.program_id(0)
    nt = pl.num_programs(0)
    tm = x_ref.shape[0]
    n_slots = TOP_K * tm
    slot = i % 2

    def fetch(tables, s):
        buf = buf_ref.at[s]
        _segment_copies(*tables, tm,
                        lambda o, d, n: pltpu.make_async_copy(_rows(y_ref, d, n), _rows(buf, o, n),
                                                              sems.at[s]))

    @pl.when(i == 0)
    def _():
        fetch((cnt_ref, off_ref, dst_ref), slot)

    @pl.when(i + 1 < nt)
    def _():
        fetch((cnt_n_ref, off_n_ref, dst_n_ref), 1 - slot)

    pltpu.make_async_copy(_rows(y_ref, 0, n_slots), buf_ref.at[slot], sems.at[slot]).wait()
    buf = buf_ref.at[slot]
    for c in range(ROW_SLABS):
        yb_ref[:, c * LANES:(c + 1) * LANES] = buf[pl.ds(c, n_slots, stride=ROW_SLABS), :].astype(BF16)
    pos = pos_ref[...]
    gates = gate_ref[...]
    j = lax.broadcasted_iota(I32, (tm, n_slots), 1)
    wsel = sum(jnp.where(pos[:, k:k + 1] == j, gates[:, k:k + 1], 0.0) for k in range(TOP_K))
    ffn = _dot(wsel.astype(BF16), yb_ref[...])
    o_ref[...] = x_ref[...] + mod_ref[0][:, 5 * D_MODEL:6 * D_MODEL] * ffn


def _combine_call(cnt, seg_off, seg_dst, x2, pos, gates, mod, y, seq, tm):
    t = x2.shape[0]
    nt = t // tm
    row = lambda i: (i, 0)
    tile = lambda i: (i, 0, 0)
    nxt = lambda i: (jnp.minimum(i + 1, nt - 1), 0, 0)
    return pl.pallas_call(
        _combine_kernel,
        grid=(nt,),
        in_specs=[
            _seg_spec(tile), _seg_spec(tile), _seg_spec(tile),
            _seg_spec(nxt), _seg_spec(nxt), _seg_spec(nxt),
            pl.BlockSpec((tm, D_MODEL), row),
            pl.BlockSpec((tm, LANES), row),
            pl.BlockSpec((tm, LANES), row),
            pl.BlockSpec((1, 1, 6 * D_MODEL), lambda i: ((i * tm) // seq, 0, 0)),
            pl.BlockSpec(memory_space=pl.ANY),
        ],
        out_specs=pl.BlockSpec((tm, D_MODEL), row),
        out_shape=jax.ShapeDtypeStruct((t, D_MODEL), F32),
        scratch_shapes=[pltpu.VMEM((2, TOP_K * tm * ROW_SLABS, LANES), F32),
                        pltpu.VMEM((TOP_K * tm, D_MODEL), BF16),
                        pltpu.SemaphoreType.DMA((2,))],
        compiler_params=_cparams(("arbitrary",)),
        name="moe_combine",
    )(cnt, seg_off, seg_dst, cnt, seg_off, seg_dst, x2, pos, gates, mod, y)


def _plan(cnt, tm_moe, n_tiles_moe):
    c = cnt[:, 0, :]
    totals = jnp.sum(c, axis=0)
    padded = (totals + tm_moe - 1) // tm_moe * tm_moe
    pad_end = jnp.cumsum(padded)
    pad_start = pad_end - padded
    seg_dst = pad_start[None, :] + jnp.cumsum(c, axis=0) - c
    seg_off = jnp.cumsum(c, axis=1) - c
    tile_start = jnp.arange(n_tiles_moe, dtype=I32) * tm_moe
    block_e = jnp.minimum(jnp.sum(tile_start[:, None] >= pad_end[None, :N_EXPERTS], axis=1),
                          N_EXPERTS - 1).astype(I32)
    n_used = (pad_end[N_EXPERTS - 1] // tm_moe).astype(I32).reshape(1)
    shape = cnt.shape
    return seg_off.astype(I32).reshape(shape), seg_dst.astype(I32).reshape(shape), block_e, n_used


def _pick(n, pref):
    for c in pref:
        if n % c == 0:
            return c
    raise ValueError(f"no tile size in {pref} divides {n}")


def _trunk(x, mods, prm):
    batch, seq, _ = x.shape
    t = batch * seq
    assert seq % RET_CHUNK == 0 and seq % ATT_BLOCK == 0
    tm_in = _pick(seq, (512, 256, 128))
    tm_route = _pick(seq, (256, 128))
    tm_step = _pick(seq, (2 * tm_route, tm_route))
    tm_moe = 512 if t * TOP_K >= 65536 else (256 if t * TOP_K >= 16384 else 128)
    n_tiles_moe = (t * TOP_K) // tm_moe + N_EXPERTS
    x2 = x.reshape(t, D_MODEL)
    xs = jnp.zeros((n_tiles_moe * tm_moe * ROW_SLABS, LANES), F32)
    tri = (jnp.arange(tm_route)[:, None] > jnp.arange(tm_route)[None, :]).astype(BF16)
    upper = (jnp.arange(LANES)[:, None] < jnp.arange(LANES)[None, :]).astype(BF16)
    depth = prm["w_in"].shape[0]
    for l in range(depth):
        mod = mods[l]
        qr, kr, vr, sg, qa, ka, va = _in_call(
            x2, mod, prm["norm_mix_g"][l], prm["w_in"], l, prm["g256"],
            prm["q_g"][l], prm["k_g"][l], seq, tm_in)
        o_r = _ret_call(qr, kr, vr, sg, prm["lgf"][l], prm["lgb"][l], prm["g128"], batch, seq)
        o_a = _attn_call(qa, ka, va, prm["sink"][l], prm["bias"], batch, seq)
        x2, hf, pos, post, gates, cnt = _route_call(
            x2, o_r, o_a, mod, prm["w_out"], l, prm["norm_ffn_g"][l],
            prm["rw"], prm["rb"][l], tri, upper, seq, tm_route, tm_step)
        seg_off, seg_dst, block_e, n_used = _plan(cnt, tm_moe, n_tiles_moe)
        xs = _dispatch_call(cnt, seg_off, seg_dst, post, hf, xs, tm_route)
        y = _moe_call(block_e, n_used, xs, prm["w_g"], prm["w_l"], prm["b_g"],
                      prm["b_l"], prm["w_dn"], prm["b_dn"], l, tm_moe)
        x2 = _combine_call(cnt, seg_off, seg_dst, x2, pos, gates, mod, y, seq, tm_route)
    return x2.reshape(batch, seq, D_MODEL)


def _prepare(ada_w, ada_b, norm_mix_g, w_in, ret_log_gamma_f, ret_log_gamma_b, q_norm_g,
             k_norm_g, attn_sink, w_out, norm_ffn_g, router_w, router_b, w_gu, b_gu, w_dn, b_dn):
    depth = w_in.shape[0]
    pad_e = LANES - N_EXPERTS
    rw = jnp.pad(router_w, ((0, 0), (0, 0), (0, pad_e)))
    rw_hi = rw.astype(BF16)
    rw_lo = (rw - rw_hi.astype(F32)).astype(BF16)
    rb = jnp.pad(router_b, ((0, 0), (0, pad_e)), constant_values=NEG_INF).reshape(depth, 1, LANES)
    lane = jnp.arange(2 * LANES)
    g256 = jnp.where((lane[:, None] // HEAD_DIM) == (lane[None, :] // HEAD_DIM),
                     1.0 / HEAD_DIM, 0.0).astype(BF16)
    g128 = g256[:LANES, :LANES]
    w_g, w_l = _deint_call(w_gu)
    return dict(
        norm_mix_g=norm_mix_g.reshape(depth, 1, D_MODEL),
        norm_ffn_g=norm_ffn_g.reshape(depth, 1, D_MODEL),
        w_in=w_in.astype(BF16),
        w_out=w_out.astype(BF16),
        lgf=ret_log_gamma_f.astype(F32),
        lgb=ret_log_gamma_b.astype(F32),
        q_g=jnp.tile(q_norm_g, (1, 2 * LANES // HEAD_DIM)).reshape(depth, 1, 2 * LANES),
        k_g=jnp.tile(k_norm_g, (1, LANES // HEAD_DIM)).reshape(depth, 1, LANES),
        sink=attn_sink.astype(F32) * LOG2E,
        rw=jnp.concatenate([rw_hi, rw_lo], axis=2), rb=rb,
        w_g=w_g,
        w_l=w_l,
        b_g=b_gu[..., 0::2].reshape(depth, N_EXPERTS, 1, D_FF),
        b_l=b_gu[..., 1::2].reshape(depth, N_EXPERTS, 1, D_FF),
        w_dn=w_dn.astype(BF16),
        b_dn=b_dn.reshape(depth, N_EXPERTS, 1, D_MODEL),
        g128=g128,
        g256=g256,
        bias=_attn_bias(),
    )


def kernel(x_prompt, x_sample, c_prompt, c_sample, ada_w, ada_b, norm_mix_g, w_in, ret_log_gamma_f, ret_log_gamma_b, q_norm_g, k_norm_g, attn_sink, w_out, norm_ffn_g, router_w, router_b, w_gu, b_gu, w_dn, b_dn):
    prm = _prepare(ada_w, ada_b, norm_mix_g, w_in, ret_log_gamma_f, ret_log_gamma_b, q_norm_g,
                   k_norm_g, attn_sink, w_out, norm_ffn_g, router_w, router_b, w_gu, b_gu,
                   w_dn, b_dn)
    bp, bs = c_prompt.shape[0], c_sample.shape[0]
    n_c = bp + bs
    n_pad = -n_c % SUBLANES
    c_all = jnp.concatenate([c_prompt, c_sample, jnp.zeros((n_pad, D_MODEL), F32)], axis=0)
    mods = _ada_call(c_all, ada_w, ada_b)
    depth = ada_w.shape[0]
    mods_p = mods[:, :bp].reshape(depth, bp, 1, 6 * D_MODEL)
    mods_s = mods[:, bp:n_c].reshape(depth, bs, 1, 6 * D_MODEL)
    y_prompt = _trunk(x_prompt, mods_p, prm)
    y_sample = _trunk(x_sample, mods_s, prm)
    return (y_prompt, y_sample)
```
